```python
import jax
import jax.numpy as jnp
from jax import lax
import numpy as np


D_MODEL = 2048
BATCH = 4
SEQ = 8192
DEPTH = 1
DEC_BATCH = 2
DEC_SEQ = 16384
PAST_LEN = 128

D_RNN = D_MODEL
RNN_BLOCKS = 16
RNN_BLOCK_W = D_RNN // RNN_BLOCKS
CONV_W = 4
CONV_LEFT = 2
LRU_C = 8.0
HEAD_DIM = 128
ATTN_PATTERNS = ((128, 1), (512, 4), (2048, 16))
N_GROUPS = 3
HEADS_PER_GROUP = 4
N_HEADS = N_GROUPS * HEADS_PER_GROUP
ATTN_W = N_HEADS * HEAD_DIM
ATTN_OUT_W = HEADS_PER_GROUP * HEAD_DIM
ATTN_BLOCK = 64
ROPE_DIM = HEAD_DIM // 4
ROPE_THETA = 500000.0
D_FF = 4 * D_MODEL
EPS = 1e-6
NEG_INF = -1e30
IN_COLS = 2 * D_RNN + 3 * ATTN_W + 2 * D_MODEL

kernel_name = 'hybrid_rglru_dilated_attn_encoder'


def rmsnorm(x, g):
    xf = x.astype(jnp.float32)
    y = xf * lax.rsqrt(jnp.mean(xf * xf, axis=-1, keepdims=True) + EPS)
    return (y * g.astype(jnp.float32)).astype(x.dtype)


def partial_rope(t):
    S = t.shape[1]
    half = ROPE_DIM // 2
    inv_freq = ROPE_THETA ** (-jnp.arange(0, ROPE_DIM, 2, dtype=jnp.float32) / ROPE_DIM)
    ang = jnp.arange(S, dtype=jnp.float32)[:, None] * inv_freq[None, :]
    cos = jnp.cos(ang)[None, :, None, :]
    sin = jnp.sin(ang)[None, :, None, :]
    tf = t.astype(jnp.float32)
    t1 = tf[..., :half]
    t2 = tf[..., half:ROPE_DIM]
    out = jnp.concatenate([t1 * cos - t2 * sin, t2 * cos + t1 * sin, tf[..., ROPE_DIM:]], axis=-1)
    return out.astype(t.dtype)


def dilated_window_attention(q, k, v, window, dil):
    B, S, H, E = q.shape
    R = window // (2 * dil)
    assert R <= ATTN_BLOCK
    L = S // dil
    nb = -(-L // ATTN_BLOCK)
    Lp = nb * ATTN_BLOCK

    def by_residue(t):
        return t.reshape(B, L, dil, H, E).transpose(0, 2, 3, 1, 4)

    qs, ks, vs = by_residue(q), by_residue(k), by_residue(v)
    qb = jnp.pad(qs, ((0, 0), (0, 0), (0, 0), (0, Lp - L), (0, 0))).reshape(B, dil, H, nb, ATTN_BLOCK, E)
    pad_kv = ((0, 0), (0, 0), (0, 0), (ATTN_BLOCK, Lp - L + ATTN_BLOCK), (0, 0))
    kp = jnp.pad(ks, pad_kv).reshape(B, dil, H, nb + 2, ATTN_BLOCK, E)
    vp = jnp.pad(vs, pad_kv).reshape(B, dil, H, nb + 2, ATTN_BLOCK, E)
    kw = jnp.concatenate([kp[:, :, :, :-2], kp[:, :, :, 1:-1], kp[:, :, :, 2:]], axis=4)
    vw = jnp.concatenate([vp[:, :, :, :-2], vp[:, :, :, 1:-1], vp[:, :, :, 2:]], axis=4)

    scale = 1.0 / float(np.sqrt(E))
    s = jnp.einsum('bdhnqe,bdhnke->bdhnqk', qb, kw, preferred_element_type=jnp.float32) * scale
    qi = jnp.arange(ATTN_BLOCK)[:, None]
    ki = jnp.arange(3 * ATTN_BLOCK)[None, :]
    off = ki - ATTN_BLOCK - qi
    kpos = jnp.arange(nb)[:, None, None] * ATTN_BLOCK - ATTN_BLOCK + ki[None]
    mask = (jnp.abs(off) <= R)[None] & (kpos >= 0) & (kpos < L)
    s = jnp.where(mask, s, NEG_INF)
    m = jnp.max(s, axis=-1, keepdims=True)
    p = jnp.exp(s - m)
    den = jnp.sum(p, axis=-1, keepdims=True)
    o = jnp.einsum('bdhnqk,bdhnke->bdhnqe', p, vw.astype(jnp.float32)) / den
    lse = (m + jnp.log(den))[..., 0]
    o = o.reshape(B, dil, H, Lp, E)[:, :, :, :L].transpose(0, 3, 1, 2, 4).reshape(B, S, H, E)
    lse = lse.reshape(B, dil, H, Lp)[:, :, :, :L].transpose(0, 3, 1, 2).reshape(B, S, H)
    return o, lse


def centered_conv(x, w, b):
    S = x.shape[1]
    xp = jnp.pad(x, ((0, 0), (CONV_LEFT, CONV_W - 1 - CONV_LEFT), (0, 0)))
    return b + sum(xp[:, j:j + S] * w[j] for j in range(CONV_W))


def _lin_comb(e1, e2):
    a1, b1 = e1
    a2, b2 = e2
    return a1 * a2, a2 * b1 + b2


def bidir_rg_lru(xc, w_a, b_a, w_i, b_i, lam):
    B, S, _ = xc.shape
    xb = xc.reshape(B, S, RNN_BLOCKS, RNN_BLOCK_W)
    xf = xc.astype(jnp.float32)
    hs = []
    for dr in range(2):
        r = jax.nn.sigmoid((jnp.einsum('bshi,hij->bshj', xb, w_a[dr]).reshape(B, S, D_RNN) + b_a[dr]).astype(jnp.float32))
        i = jax.nn.sigmoid((jnp.einsum('bshi,hij->bshj', xb, w_i[dr]).reshape(B, S, D_RNN) + b_i[dr]).astype(jnp.float32))
        log_a = -LRU_C * r * jax.nn.softplus(-lam[dr].astype(jnp.float32))
        a = jnp.exp(log_a)
        u = jnp.sqrt(-jnp.expm1(2.0 * log_a)) * (i * xf)
        _, h = lax.associative_scan(_lin_comb, (a, u), axis=1, reverse=(dr == 1))
        hs.append(h)
    return hs[0] + hs[1]


def encoder_layer(x, mix_g, w_in, conv_w, conv_b, w_a, b_a, w_i, b_i, lam,
                  w_rnn_proj, w_attn_proj, w_out, mlp_g, w_up, w_down):
    B, S, _ = x.shape
    xn = rmsnorm(x, mix_g)
    proj = xn @ w_in
    c0 = 2 * D_RNN
    cuts = [D_RNN, c0, c0 + ATTN_W, c0 + 2 * ATTN_W, c0 + 3 * ATTN_W, c0 + 3 * ATTN_W + D_MODEL]
    x_rnn, g_rnn, q, k, v, gate_rnn, gate_attn = jnp.split(proj, cuts, axis=-1)

    xc = centered_conv(x_rnn, conv_w, conv_b)
    h = bidir_rg_lru(xc, w_a, b_a, w_i, b_i, lam)
    y_rnn = (h * jax.nn.gelu(g_rnn.astype(jnp.float32))).astype(x.dtype) @ w_rnn_proj

    q = partial_rope(q.reshape(B, S, N_HEADS, HEAD_DIM))
    k = partial_rope(k.reshape(B, S, N_HEADS, HEAD_DIM))
    v = v.reshape(B, S, N_HEADS, HEAD_DIM)
    outs, lses = [], []
    for g, (window, dil) in enumerate(ATTN_PATTERNS):
        hsl = slice(g * HEADS_PER_GROUP, (g + 1) * HEADS_PER_GROUP)
        o, lse = dilated_window_attention(q[:, :, hsl], k[:, :, hsl], v[:, :, hsl], window, dil)
        outs.append(o)
        lses.append(lse)
    wts = jax.nn.softmax(jnp.stack(lses), axis=0)
    attn = jnp.einsum('gbsh,gbshe->bshe', wts, jnp.stack(outs)).reshape(B, S, ATTN_OUT_W).astype(x.dtype)
    y_attn = attn @ w_attn_proj

    mixed = jax.nn.sigmoid(gate_rnn) * y_rnn + jax.nn.sigmoid(gate_attn) * y_attn
    h1 = x + mixed @ w_out

    u = rmsnorm(h1, mlp_g) @ w_up
    return h1 + jnp.square(jax.nn.relu(u)) @ w_down


def setup_inputs(seed: int = 0) -> dict:
    key = jax.random.key(seed)
    ks = jax.random.split(key, 20)
    f32 = jnp.float32

    def nrm(k, shape, fan_in):
        return jax.random.normal(k, shape, f32) * (fan_in ** -0.5)

    u = jax.random.uniform(ks[9], (DEPTH, 2, D_RNN), f32, minval=0.9, maxval=0.999)
    s = u ** (1.0 / LRU_C)
    lam = jnp.log(s) - jnp.log1p(-s)
    return {
        'x_prompt': jax.random.normal(ks[0], (BATCH, SEQ, D_MODEL), f32),
        'x_sample': jax.random.normal(ks[1], (DEC_BATCH, DEC_SEQ, D_MODEL), f32),
        'mix_norm_g': 1.0 + 0.02 * jax.random.normal(ks[2], (DEPTH, D_MODEL), f32),
        'w_in': nrm(ks[3], (DEPTH, D_MODEL, IN_COLS), D_MODEL),
        'conv_w': nrm(ks[4], (DEPTH, CONV_W, D_RNN), CONV_W),
        'conv_b': 0.01 * jax.random.normal(ks[5], (DEPTH, D_RNN), f32),
        'lru_w_a': nrm(ks[6], (DEPTH, 2, RNN_BLOCKS, RNN_BLOCK_W, RNN_BLOCK_W), RNN_BLOCK_W),
        'lru_b_a': 0.01 * jax.random.normal(ks[7], (DEPTH, 2, D_RNN), f32),
        'lru_w_i': nrm(ks[8], (DEPTH, 2, RNN_BLOCKS, RNN_BLOCK_W, RNN_BLOCK_W), RNN_BLOCK_W),
        'lru_b_i': 0.01 * jax.random.normal(ks[10], (DEPTH, 2, D_RNN), f32),
        'lru_lambda': lam,
        'w_rnn_proj': nrm(ks[11], (DEPTH, D_RNN, D_MODEL), D_RNN),
        'w_attn_proj': nrm(ks[12], (DEPTH, ATTN_OUT_W, D_MODEL), ATTN_OUT_W),
        'w_out': nrm(ks[13], (DEPTH, D_MODEL, D_MODEL), D_MODEL),
        'mlp_norm_g': 1.0 + 0.02 * jax.random.normal(ks[14], (DEPTH, D_MODEL), f32),
        'w_up': nrm(ks[15], (DEPTH, D_MODEL, D_FF), D_MODEL),
        'w_down': nrm(ks[16], (DEPTH, D_FF, D_MODEL), D_FF),
        'final_norm_g': 1.0 + 0.02 * jax.random.normal(ks[17], (D_MODEL,), f32),
    }


def reference(x_prompt, x_sample, mix_norm_g, w_in, conv_w, conv_b, lru_w_a, lru_b_a,
              lru_w_i, lru_b_i, lru_lambda, w_rnn_proj, w_attn_proj, w_out,
              mlp_norm_g, w_up, w_down, final_norm_g):
    def run(x):
        for l in range(DEPTH):
            x = encoder_layer(x, mix_norm_g[l], w_in[l], conv_w[l], conv_b[l],
                              lru_w_a[l], lru_b_a[l], lru_w_i[l], lru_b_i[l], lru_lambda[l],
                              w_rnn_proj[l], w_attn_proj[l], w_out[l],
                              mlp_norm_g[l], w_up[l], w_down[l])
        return rmsnorm(x, final_norm_g)

    y_prompt = run(x_prompt)
    y_sample = run(x_sample)
    return (y_prompt, y_sample)
```

```python
import functools

import numpy as np
import jax
import jax.numpy as jnp
from jax import lax
from jax.experimental import pallas as pl
from jax.experimental.pallas import tpu as pltpu

F32 = jnp.float32
BF16 = jnp.bfloat16

LANES = 128
SUBLANES = 8
VMEM_LIMIT = 56 * 1024 * 1024

D_MODEL = 2048
D_RNN = D_MODEL
RNN_BLOCKS = 16
CONV_W = 4
LRU_C = 8.0
HEAD_DIM = 128
ATTN_PATTERNS = ((128, 1), (512, 4), (2048, 16))
HEADS_PER_GROUP = 4
N_HEADS = len(ATTN_PATTERNS) * HEADS_PER_GROUP
ATTN_W = N_HEADS * HEAD_DIM
ATTN_OUT_W = HEADS_PER_GROUP * HEAD_DIM
ATTN_RADIUS = 64
ROPE_DIM = HEAD_DIM // 4
ROPE_THETA = 500000.0
D_FF = 4 * D_MODEL
EPS = 1e-6
NEG_INF = -1e30

CB_XRNN = 0
CB_GRNN = 16
CB_GATE_RNN = 32
CB_GATE_ATTN = 48
CB_Q = 64
CB_K = CB_Q + N_HEADS
CB_V = CB_K + N_HEADS
IN_BLOCKS = CB_V + N_HEADS
IN_COLS = IN_BLOCKS * LANES


def _tiles(seq):
    return dict(
        proj_tm=min(1024, seq), proj_tn=512,
        rnn_chunk=min(256, seq),
        attn_tq=512, attn_sub=128,
        merge_tm=min(256, seq),
        mlp_tm=min(512, seq), mlp_tf=512,
    )


def _params(*sem):
    return pltpu.CompilerParams(dimension_semantics=sem, vmem_limit_bytes=VMEM_LIMIT)


def _rms(xf, g):
    return xf * lax.rsqrt(jnp.mean(xf * xf, axis=-1, keepdims=True) + EPS) * g


def _inproj_kernel(x_ref, g_ref, w_ref, cos_ref, sin_ref, o_ref, xn_ref, *, rope_lo, rope_hi):
    n = pl.program_id(1)

    @pl.when(n == 0)
    def _():
        xn_ref[...] = _rms(x_ref[...], g_ref[...]).astype(BF16)

    acc = jnp.dot(xn_ref[...], w_ref[...], preferred_element_type=F32)
    nblk = o_ref.shape[0]
    is_rope = jnp.logical_and(n >= rope_lo, n < rope_hi)

    @pl.when(is_rope)
    def _():
        c = cos_ref[...]
        s = sin_ref[...]
        lane = lax.broadcasted_iota(jnp.int32, c.shape, 1)
        first_half = lane < ROPE_DIM // 2
        for j in range(nblk):
            t = acc[:, j * LANES:(j + 1) * LANES]
            sw = jnp.where(first_half, pltpu.roll(t, LANES - ROPE_DIM // 2, 1), pltpu.roll(t, ROPE_DIM // 2, 1))
            o_ref[j] = t * c + sw * s

    @pl.when(jnp.logical_not(is_rope))
    def _():
        for j in range(nblk):
            o_ref[j] = acc[:, j * LANES:(j + 1) * LANES]


def _inproj(x2, g, w_in_p, cos_t, sin_t, seq, cfg):
    T = x2.shape[0]
    tm, tn = cfg["proj_tm"], cfg["proj_tn"]
    nb = tn // LANES
    tiles_per_seq = seq // tm
    kern = functools.partial(_inproj_kernel, rope_lo=CB_Q // nb, rope_hi=CB_V // nb)
    return pl.pallas_call(
        kern,
        grid=(T // tm, IN_COLS // tn),
        in_specs=[
            pl.BlockSpec((tm, D_MODEL), lambda m, n: (m, 0)),
            pl.BlockSpec((1, D_MODEL), lambda m, n: (0, 0)),
            pl.BlockSpec((D_MODEL, tn), lambda m, n: (0, n)),
            pl.BlockSpec((tm, LANES), lambda m, n: (m % tiles_per_seq, 0)),
            pl.BlockSpec((tm, LANES), lambda m, n: (m % tiles_per_seq, 0)),
        ],
        out_specs=pl.BlockSpec((nb, tm, LANES), lambda m, n: (n, m, 0)),
        out_shape=jax.ShapeDtypeStruct((IN_BLOCKS, T, LANES), F32),
        scratch_shapes=[pltpu.VMEM((tm, D_MODEL), BF16)],
        compiler_params=_params("parallel", "arbitrary"),
        name="inproj",
    )(x2, g, w_in_p, cos_t, sin_t)


def _scan_chunk(a, u, carry, reverse):
    ch = a.shape[0]
    row = lax.broadcasted_iota(jnp.int32, a.shape, 0) % SUBLANES
    for d in (1, 2, 4):
        if reverse:
            shift, valid = ch - d, row < SUBLANES - d
        else:
            shift, valid = d, row >= d
        a_s = pltpu.roll(a, shift, 0)
        u_s = pltpu.roll(u, shift, 0)
        u = jnp.where(valid, a * u_s + u, u)
        a = jnp.where(valid, a * a_s, a)
    groups = ch // SUBLANES
    order = range(groups - 1, -1, -1) if reverse else range(groups)
    hs = [None] * groups
    for gi in order:
        sl = slice(gi * SUBLANES, (gi + 1) * SUBLANES)
        hg = a[sl] * carry + u[sl]
        hs[gi] = hg
        carry = hg[0:1] if reverse else hg[SUBLANES - 1:SUBLANES]
    return jnp.concatenate(hs, axis=0), carry


def _rnn_kernel(x_ref, cw_ref, cb_ref, wg_ref, bg_ref, lam_ref, h_ref, *, seq, chunk):
    n_chunks = seq // chunk
    cw = cw_ref[...]
    cb = cb_ref[...]
    ext_rows = chunk + 2 * SUBLANES

    def gates(c, dr):
        c0 = pl.multiple_of(c * chunk, chunk)
        x = x_ref[0, pl.ds(c0, chunk), :]
        prev = x_ref[0, pl.ds(pl.multiple_of(jnp.maximum(c0 - SUBLANES, 0), SUBLANES), SUBLANES), :]
        nxt = x_ref[0, pl.ds(pl.multiple_of(jnp.minimum(c0 + chunk, seq - SUBLANES), SUBLANES), SUBLANES), :]
        prev = jnp.where(c == 0, 0.0, prev)
        nxt = jnp.where(c == n_chunks - 1, 0.0, nxt)
        ext = jnp.concatenate([prev, x, nxt], axis=0)
        mid = slice(SUBLANES, SUBLANES + chunk)
        xc = (cb + pltpu.roll(ext, 2, 0)[mid] * cw[0:1] + pltpu.roll(ext, 1, 0)[mid] * cw[1:2]
              + x * cw[2:3] + pltpu.roll(ext, ext_rows - 1, 0)[mid] * cw[3:4])
        gz = jnp.dot(xc.astype(BF16), wg_ref[dr, 0], preferred_element_type=F32) + bg_ref[dr, 0]
        r = jax.nn.sigmoid(gz[:, :LANES])
        i = jax.nn.sigmoid(gz[:, LANES:])
        z = -lam_ref[dr, 0]
        softplus = jnp.maximum(z, 0.0) + jnp.log1p(jnp.exp(-jnp.abs(z)))
        log_a = -LRU_C * r * softplus
        a = jnp.exp(log_a)
        u = jnp.sqrt(-jnp.tanh(log_a) * (a * a + 1.0)) * (i * xc)
        return c0, a, u

    def fwd(c, carry):
        c0, a, u = gates(c, 0)
        h, carry = _scan_chunk(a, u, carry, reverse=False)
        h_ref[0, pl.ds(c0, chunk), :] = h
        return carry

    def bwd(k, carry):
        c0, a, u = gates(n_chunks - 1 - k, 1)
        h, carry = _scan_chunk(a, u, carry, reverse=True)
        h_ref[0, pl.ds(c0, chunk), :] += h
        return carry

    zero = jnp.zeros((1, LANES), F32)
    lax.fori_loop(0, n_chunks, fwd, zero)
    lax.fori_loop(0, n_chunks, bwd, zero)


def _rnn(proj, conv_w, conv_b, wg, bg, lam, batch, seq, cfg):
    T = proj.shape[1]
    kern = functools.partial(_rnn_kernel, seq=seq, chunk=cfg["rnn_chunk"])
    return pl.pallas_call(
        kern,
        grid=(batch, RNN_BLOCKS),
        in_specs=[
            pl.BlockSpec((1, seq, LANES), lambda b, j: (CB_XRNN + j, b, 0)),
            pl.BlockSpec((CONV_W, LANES), lambda b, j: (0, j)),
            pl.BlockSpec((1, LANES), lambda b, j: (0, j)),
            pl.BlockSpec((2, 1, LANES, 2 * LANES), lambda b, j: (0, j, 0, 0)),
            pl.BlockSpec((2, 1, 1, 2 * LANES), lambda b, j: (0, j, 0, 0)),
            pl.BlockSpec((2, 1, 1, LANES), lambda b, j: (0, j, 0, 0)),
        ],
        out_specs=pl.BlockSpec((1, seq, LANES), lambda b, j: (j, b, 0)),
        out_shape=jax.ShapeDtypeStruct((RNN_BLOCKS, T, LANES), F32),
        compiler_params=_params("parallel", "parallel"),
        name="rglru",
    )(proj, conv_w, conv_b, wg, bg, lam)


def _attn_kernel(q_ref, kp_ref, km_ref, kn_ref, vp_ref, vm_ref, vn_ref, o_ref, lse_ref, *, sub_len, tq, sub):
    i = pl.program_id(3)
    q = q_ref[0].astype(BF16)
    kext = jnp.concatenate([kp_ref[0], km_ref[0], kn_ref[0]], axis=0).astype(BF16)
    vext = jnp.concatenate([vp_ref[0], vm_ref[0], vn_ref[0]], axis=0).astype(BF16)
    nk = sub + 2 * ATTN_RADIUS
    qi = lax.broadcasted_iota(jnp.int32, (sub, nk), 0)
    kj = lax.broadcasted_iota(jnp.int32, (sub, nk), 1)
    band = jnp.abs(kj - ATTN_RADIUS - qi) <= ATTN_RADIUS
    scale = 1.0 / float(np.sqrt(HEAD_DIM))
    for sb in range(tq // sub):
        r0 = sb * sub
        s = lax.dot_general(q[r0:r0 + sub], kext[r0:r0 + nk], (((1,), (1,)), ((), ())),
                            preferred_element_type=F32) * scale
        kpos = i * tq + (r0 - ATTN_RADIUS) + kj
        mask = band & (kpos >= 0) & (kpos < sub_len)
        s = jnp.where(mask, s, NEG_INF)
        m = jnp.max(s, axis=-1, keepdims=True)
        p = jnp.exp(s - m)
        den = jnp.sum(p, axis=-1, keepdims=True)
        o = jnp.dot(p.astype(BF16), vext[r0:r0 + nk], preferred_element_type=F32) / den
        o_ref[r0:r0 + sub, :] = o
        lse_ref[r0:r0 + sub, :] = jnp.broadcast_to(m + jnp.log(den), (sub, LANES))


def _attention_group(proj, g, dil, batch, seq, cfg):
    T = proj.shape[1]
    sub_len = seq // dil
    tq = min(cfg["attn_tq"], sub_len)
    sub = min(cfg["attn_sub"], tq)
    n_t = sub_len // tq
    halo_per_tile = tq // ATTN_RADIUS
    n_halo_blocks = (T // dil) // ATTN_RADIUS
    pv = proj.reshape(IN_BLOCKS, T // dil, dil * LANES)
    h0 = g * HEADS_PER_GROUP

    def main(cb):
        return pl.BlockSpec((1, tq, LANES), lambda b, r, h, i: (cb + h0 + h, b * n_t + i, r))

    def prev(cb):
        return pl.BlockSpec((1, ATTN_RADIUS, LANES),
                            lambda b, r, h, i: (cb + h0 + h, jnp.maximum((b * n_t + i) * halo_per_tile - 1, 0), r))

    def nxt(cb):
        return pl.BlockSpec((1, ATTN_RADIUS, LANES),
                            lambda b, r, h, i: (cb + h0 + h,
                                                jnp.minimum((b * n_t + i + 1) * halo_per_tile, n_halo_blocks - 1), r))

    out_spec = pl.BlockSpec((tq, LANES), lambda b, r, h, i: (b * n_t + i, r * HEADS_PER_GROUP + h))
    out_sds = jax.ShapeDtypeStruct((T // dil, dil * ATTN_OUT_W), F32)
    kern = functools.partial(_attn_kernel, sub_len=sub_len, tq=tq, sub=sub)
    o, lse = pl.pallas_call(
        kern,
        grid=(batch, dil, HEADS_PER_GROUP, n_t),
        in_specs=[main(CB_Q), prev(CB_K), main(CB_K), nxt(CB_K), prev(CB_V), main(CB_V), nxt(CB_V)],
        out_specs=[out_spec, out_spec],
        out_shape=[out_sds, out_sds],
        compiler_params=_params("parallel", "parallel", "parallel", "parallel"),
        name=f"attn_g{g}",
    )(pv, pv, pv, pv, pv, pv, pv)
    return o.reshape(T, ATTN_OUT_W), lse.reshape(T, ATTN_OUT_W)


def _merge_kernel(h_ref, g_ref, gr_ref, ga_ref, o0_ref, l0_ref, o1_ref, l1_ref, o2_ref, l2_ref,
                  wr_ref, wa_ref, out_ref):
    hg = [(h_ref[j] * jax.nn.gelu(g_ref[j])).astype(BF16) for j in range(RNN_BLOCKS)]
    y_rnn = jnp.dot(jnp.concatenate(hg, axis=-1), wr_ref[...], preferred_element_type=F32)
    pieces = []
    for h in range(HEADS_PER_GROUP):
        sl = slice(h * LANES, (h + 1) * LANES)
        ls = [l_ref[:, sl] for l_ref in (l0_ref, l1_ref, l2_ref)]
        os_ = [o_ref[:, sl] for o_ref in (o0_ref, o1_ref, o2_ref)]
        mx = jnp.maximum(jnp.maximum(ls[0], ls[1]), ls[2])
        es = [jnp.exp(l - mx) for l in ls]
        den = es[0] + es[1] + es[2]
        pieces.append(((es[0] * os_[0] + es[1] * os_[1] + es[2] * os_[2]) / den).astype(BF16))
    y_attn = jnp.dot(jnp.concatenate(pieces, axis=-1), wa_ref[...], preferred_element_type=F32)
    for j in range(RNN_BLOCKS):
        sl = slice(j * LANES, (j + 1) * LANES)
        mixed = jax.nn.sigmoid(gr_ref[j]) * y_rnn[:, sl] + jax.nn.sigmoid(ga_ref[j]) * y_attn[:, sl]
        out_ref[:, sl] = mixed.astype(BF16)


def _merge(proj, hrnn, attn_outs, w_rnn_proj, w_attn_proj, cfg):
    T = proj.shape[1]
    tm = cfg["merge_tm"]
    blk16 = lambda cb: pl.BlockSpec((RNN_BLOCKS, tm, LANES), lambda m: (cb // RNN_BLOCKS, m, 0))
    tok = lambda w: pl.BlockSpec((tm, w), lambda m: (m, 0))
    const = lambda a: pl.BlockSpec(a.shape, lambda m: (0, 0), pipeline_mode=pl.Buffered(1))
    flat = [a for pair in attn_outs for a in pair]
    return pl.pallas_call(
        _merge_kernel,
        grid=(T // tm,),
        in_specs=[blk16(0), blk16(CB_GRNN), blk16(CB_GATE_RNN), blk16(CB_GATE_ATTN)]
                 + [tok(ATTN_OUT_W)] * 6 + [const(w_rnn_proj), const(w_attn_proj)],
        out_specs=tok(D_MODEL),
        out_shape=jax.ShapeDtypeStruct((T, D_MODEL), BF16),
        compiler_params=_params("parallel"),
        name="merge",
    )(hrnn, proj, proj, proj, *flat, w_rnn_proj, w_attn_proj)


def _mlp_kernel(x_ref, mix_ref, wo_ref, g_ref, fg_ref, wu_ref, wd_ref, o_ref, h1_ref, xn_ref, acc_ref):
    f = pl.program_id(1)

    @pl.when(f == 0)
    def _():
        h1 = x_ref[...] + jnp.dot(mix_ref[...], wo_ref[...], preferred_element_type=F32)
        h1_ref[...] = h1
        xn_ref[...] = _rms(h1, g_ref[...]).astype(BF16)
        acc_ref[...] = jnp.zeros_like(acc_ref)

    u = jnp.dot(xn_ref[...], wu_ref[...], preferred_element_type=F32)
    act = jnp.square(jnp.maximum(u, 0.0)).astype(BF16)
    acc_ref[...] += jnp.dot(act, wd_ref[...], preferred_element_type=F32)

    @pl.when(f == pl.num_programs(1) - 1)
    def _():
        o_ref[...] = _rms(h1_ref[...] + acc_ref[...], fg_ref[...])


def _mlp(x2, mixed, w_out, mlp_g, final_g, w_up, w_down, cfg):
    T = x2.shape[0]
    tm, tf = cfg["mlp_tm"], cfg["mlp_tf"]
    tok = pl.BlockSpec((tm, D_MODEL), lambda m, f: (m, 0))
    vec = pl.BlockSpec((1, D_MODEL), lambda m, f: (0, 0))
    return pl.pallas_call(
        _mlp_kernel,
        grid=(T // tm, D_FF // tf),
        in_specs=[
            tok, tok,
            pl.BlockSpec((D_MODEL, D_MODEL), lambda m, f: (0, 0), pipeline_mode=pl.Buffered(1)),
            vec, vec,
            pl.BlockSpec((D_MODEL, tf), lambda m, f: (0, f)),
            pl.BlockSpec((tf, D_MODEL), lambda m, f: (f, 0)),
        ],
        out_specs=tok,
        out_shape=jax.ShapeDtypeStruct((T, D_MODEL), F32),
        scratch_shapes=[pltpu.VMEM((tm, D_MODEL), F32), pltpu.VMEM((tm, D_MODEL), BF16),
                        pltpu.VMEM((tm, D_MODEL), F32)],
        compiler_params=_params("parallel", "arbitrary"),
        name="mlp",
    )(x2, mixed, w_out, mlp_g, final_g, w_up, w_down)


def _rope_tables(seq):
    half = ROPE_DIM // 2
    inv_freq = ROPE_THETA ** (-jnp.arange(0, ROPE_DIM, 2, dtype=F32) / ROPE_DIM)
    ang = jnp.arange(seq, dtype=F32)[:, None] * inv_freq[None, :]
    cos, sin = jnp.cos(ang), jnp.sin(ang)
    rest = HEAD_DIM - ROPE_DIM
    cos_t = jnp.concatenate([cos, cos, jnp.ones((seq, rest), F32)], axis=-1)
    sin_t = jnp.concatenate([-sin, sin, jnp.zeros((seq, rest), F32)], axis=-1)
    assert cos_t.shape[1] == LANES and half * 2 == ROPE_DIM
    return cos_t, sin_t


def _layer(x, wts):
    batch, seq, _ = x.shape
    cfg = _tiles(seq)
    x2 = x.reshape(batch * seq, D_MODEL)
    cos_t, sin_t = _rope_tables(seq)
    proj = _inproj(x2, wts["mix_g"], wts["w_in"], cos_t, sin_t, seq, cfg)
    hrnn = _rnn(proj, wts["conv_w"], wts["conv_b"], wts["wg"], wts["bg"], wts["lam"], batch, seq, cfg)
    attn_outs = [_attention_group(proj, g, dil, batch, seq, cfg) for g, (_, dil) in enumerate(ATTN_PATTERNS)]
    mixed = _merge(proj, hrnn, attn_outs, wts["w_rnn_proj"], wts["w_attn_proj"], cfg)
    y = _mlp(x2, mixed, wts["w_out"], wts["mlp_g"], wts["final_g"], wts["w_up"], wts["w_down"], cfg)
    return y.reshape(batch, seq, D_MODEL)


def kernel(x_prompt, x_sample, mix_norm_g, w_in, conv_w, conv_b, lru_w_a, lru_b_a, lru_w_i, lru_b_i,
           lru_lambda, w_rnn_proj, w_attn_proj, w_out, mlp_norm_g, w_up, w_down, final_norm_g):
    assert mix_norm_g.shape[0] == 1, "single-layer stack"
    for window, dil in ATTN_PATTERNS:
        assert window // (2 * dil) == ATTN_RADIUS
    l = 0
    c0 = 2 * D_RNN
    c1 = c0 + 3 * ATTN_W
    w = w_in[l]
    w_in_p = jnp.concatenate([w[:, :c0], w[:, c1:], w[:, c0:c1]], axis=1).astype(BF16)
    nb = RNN_BLOCKS
    bw = D_RNN // nb
    wts = dict(
        mix_g=mix_norm_g[l][None, :],
        w_in=w_in_p,
        conv_w=conv_w[l],
        conv_b=conv_b[l][None, :],
        wg=jnp.concatenate([lru_w_a[l], lru_w_i[l]], axis=-1).astype(BF16),
        bg=jnp.concatenate([lru_b_a[l].reshape(2, nb, 1, bw), lru_b_i[l].reshape(2, nb, 1, bw)], axis=-1),
        lam=lru_lambda[l].reshape(2, nb, 1, bw),
        w_rnn_proj=w_rnn_proj[l].astype(BF16),
        w_attn_proj=w_attn_proj[l].astype(BF16),
        w_out=w_out[l].astype(BF16),
        mlp_g=mlp_norm_g[l][None, :],
        final_g=final_norm_g[None, :],
        w_up=w_up[l].astype(BF16),
        w_down=w_down[l].astype(BF16),
    )
    return (_layer(x_prompt, wts), _layer(x_sample, wts))
```

```python
import functools

import numpy as np
import jax
import jax.numpy as jnp
from jax import lax
from jax.experimental import pallas as pl
from jax.experimental.pallas import tpu as pltpu

F32 = jnp.float32
BF16 = jnp.bfloat16

LANES = 128
SUBLANES = 8
VMEM_LIMIT = 56 * 1024 * 1024

D_MODEL = 2048
D_RNN = D_MODEL
RNN_BLOCKS = 16
CONV_W = 4
LRU_C = 8.0
HEAD_DIM = 128
ATTN_PATTERNS = ((128, 1), (512, 4), (2048, 16))
HEADS_PER_GROUP = 4
N_HEADS = len(ATTN_PATTERNS) * HEADS_PER_GROUP
ATTN_W = N_HEADS * HEAD_DIM
ATTN_OUT_W = HEADS_PER_GROUP * HEAD_DIM
ATTN_RADIUS = 64
ROPE_DIM = HEAD_DIM // 4
ROPE_THETA = 500000.0
D_FF = 4 * D_MODEL
EPS = 1e-6
NEG_INF = -1e30

CB_XRNN = 0
CB_GRNN = 16
CB_GATE_RNN = 32
CB_GATE_ATTN = 48
CB_Q = 64
CB_K = CB_Q + N_HEADS
CB_V = CB_K + N_HEADS
IN_BLOCKS = CB_V + N_HEADS
IN_COLS = IN_BLOCKS * LANES


def _tiles(seq):
    return dict(
        proj_tm=min(1024, seq), proj_tn=512,
        rnn_chunk=min(256, seq),
        attn_tokens=2048, attn_sub=128,
        merge_tm=min(256, seq),
        mlp_tm=min(512, seq), mlp_tf=512,
    )


def _params(*sem):
    return pltpu.CompilerParams(dimension_semantics=sem, vmem_limit_bytes=VMEM_LIMIT)


def _rms(xf, g):
    return xf * lax.rsqrt(jnp.mean(xf * xf, axis=-1, keepdims=True) + EPS) * g


def _inproj_kernel(x_ref, g_ref, w_ref, cos_ref, sin_ref, o_ref, xn_ref, *, rope_lo, rope_hi):
    n = pl.program_id(1)

    @pl.when(n == 0)
    def _():
        xn_ref[...] = _rms(x_ref[...], g_ref[...]).astype(BF16)

    acc = jnp.dot(xn_ref[...], w_ref[...], preferred_element_type=F32)
    nblk = o_ref.shape[0]
    is_rope = jnp.logical_and(n >= rope_lo, n < rope_hi)

    @pl.when(is_rope)
    def _():
        c = cos_ref[...]
        s = sin_ref[...]
        lane = lax.broadcasted_iota(jnp.int32, c.shape, 1)
        first_half = lane < ROPE_DIM // 2
        for j in range(nblk):
            t = acc[:, j * LANES:(j + 1) * LANES]
            sw = jnp.where(first_half, pltpu.roll(t, LANES - ROPE_DIM // 2, 1), pltpu.roll(t, ROPE_DIM // 2, 1))
            o_ref[j] = t * c + sw * s

    @pl.when(jnp.logical_not(is_rope))
    def _():
        for j in range(nblk):
            o_ref[j] = acc[:, j * LANES:(j + 1) * LANES]


def _inproj(x2, g, w_in_p, cos_t, sin_t, seq, cfg):
    T = x2.shape[0]
    tm, tn = cfg["proj_tm"], cfg["proj_tn"]
    nb = tn // LANES
    tiles_per_seq = seq // tm
    kern = functools.partial(_inproj_kernel, rope_lo=CB_Q // nb, rope_hi=CB_V // nb)
    return pl.pallas_call(
        kern,
        grid=(T // tm, IN_COLS // tn),
        in_specs=[
            pl.BlockSpec((tm, D_MODEL), lambda m, n: (m, 0)),
            pl.BlockSpec((1, D_MODEL), lambda m, n: (0, 0)),
            pl.BlockSpec((D_MODEL, tn), lambda m, n: (0, n)),
            pl.BlockSpec((tm, LANES), lambda m, n: (m % tiles_per_seq, 0)),
            pl.BlockSpec((tm, LANES), lambda m, n: (m % tiles_per_seq, 0)),
        ],
        out_specs=pl.BlockSpec((nb, tm, LANES), lambda m, n: (n, m, 0)),
        out_shape=jax.ShapeDtypeStruct((IN_BLOCKS, T, LANES), F32),
        scratch_shapes=[pltpu.VMEM((tm, D_MODEL), BF16)],
        compiler_params=_params("parallel", "arbitrary"),
        name="inproj",
    )(x2, g, w_in_p, cos_t, sin_t)


def _scan_chunk(a, u, carry, reverse):
    ch = a.shape[0]
    row = lax.broadcasted_iota(jnp.int32, a.shape, 0) % SUBLANES
    for d in (1, 2, 4):
        if reverse:
            shift, valid = ch - d, row < SUBLANES - d
        else:
            shift, valid = d, row >= d
        a_s = pltpu.roll(a, shift, 0)
        u_s = pltpu.roll(u, shift, 0)
        u = jnp.where(valid, a * u_s + u, u)
        a = jnp.where(valid, a * a_s, a)
    groups = ch // SUBLANES
    order = range(groups - 1, -1, -1) if reverse else range(groups)
    hs = [None] * groups
    for gi in order:
        sl = slice(gi * SUBLANES, (gi + 1) * SUBLANES)
        hg = a[sl] * carry + u[sl]
        hs[gi] = hg
        carry = hg[0:1] if reverse else hg[SUBLANES - 1:SUBLANES]
    return jnp.concatenate(hs, axis=0), carry


def _rnn_kernel(x_ref, cw_ref, cb_ref, wg_ref, bg_ref, lam_ref, h_ref, *, seq, chunk):
    n_chunks = seq // chunk
    cw = cw_ref[...]
    cb = cb_ref[...]
    ext_rows = chunk + 2 * SUBLANES

    def gates(c, dr):
        c0 = pl.multiple_of(c * chunk, chunk)
        x = x_ref[0, pl.ds(c0, chunk), :]
        prev = x_ref[0, pl.ds(pl.multiple_of(jnp.maximum(c0 - SUBLANES, 0), SUBLANES), SUBLANES), :]
        nxt = x_ref[0, pl.ds(pl.multiple_of(jnp.minimum(c0 + chunk, seq - SUBLANES), SUBLANES), SUBLANES), :]
        prev = jnp.where(c == 0, 0.0, prev)
        nxt = jnp.where(c == n_chunks - 1, 0.0, nxt)
        ext = jnp.concatenate([prev, x, nxt], axis=0)
        mid = slice(SUBLANES, SUBLANES + chunk)
        xc = (cb + pltpu.roll(ext, 2, 0)[mid] * cw[0:1] + pltpu.roll(ext, 1, 0)[mid] * cw[1:2]
              + x * cw[2:3] + pltpu.roll(ext, ext_rows - 1, 0)[mid] * cw[3:4])
        gz = jnp.dot(xc.astype(BF16), wg_ref[dr, 0], preferred_element_type=F32) + bg_ref[dr, 0]
        r = jax.nn.sigmoid(gz[:, :LANES])
        i = jax.nn.sigmoid(gz[:, LANES:])
        z = -lam_ref[dr, 0]
        softplus = jnp.maximum(z, 0.0) + jnp.log1p(jnp.exp(-jnp.abs(z)))
        log_a = -LRU_C * r * softplus
        a = jnp.exp(log_a)
        u = jnp.sqrt(-jnp.tanh(log_a) * (a * a + 1.0)) * (i * xc)
        return c0, a, u

    def fwd(c, carry):
        c0, a, u = gates(c, 0)
        h, carry = _scan_chunk(a, u, carry, reverse=False)
        h_ref[0, pl.ds(c0, chunk), :] = h
        return carry

    def bwd(k, carry):
        c0, a, u = gates(n_chunks - 1 - k, 1)
        h, carry = _scan_chunk(a, u, carry, reverse=True)
        h_ref[0, pl.ds(c0, chunk), :] += h
        return carry

    zero = jnp.zeros((1, LANES), F32)
    lax.fori_loop(0, n_chunks, fwd, zero)
    lax.fori_loop(0, n_chunks, bwd, zero)


def _rnn(proj, conv_w, conv_b, wg, bg, lam, batch, seq, cfg):
    T = proj.shape[1]
    kern = functools.partial(_rnn_kernel, seq=seq, chunk=cfg["rnn_chunk"])
    return pl.pallas_call(
        kern,
        grid=(batch, RNN_BLOCKS),
        in_specs=[
            pl.BlockSpec((1, seq, LANES), lambda b, j: (CB_XRNN + j, b, 0)),
            pl.BlockSpec((CONV_W, LANES), lambda b, j: (0, j)),
            pl.BlockSpec((1, LANES), lambda b, j: (0, j)),
            pl.BlockSpec((2, 1, LANES, 2 * LANES), lambda b, j: (0, j, 0, 0)),
            pl.BlockSpec((2, 1, 1, 2 * LANES), lambda b, j: (0, j, 0, 0)),
            pl.BlockSpec((2, 1, 1, LANES), lambda b, j: (0, j, 0, 0)),
        ],
        out_specs=pl.BlockSpec((1, seq, LANES), lambda b, j: (j, b, 0)),
        out_shape=jax.ShapeDtypeStruct((RNN_BLOCKS, T, LANES), F32),
        compiler_params=_params("parallel", "parallel"),
        name="rglru",
    )(proj, conv_w, conv_b, wg, bg, lam)


def _attn_kernel(q_ref, kp_ref, km_ref, kn_ref, vp_ref, vm_ref, vn_ref, o_ref, lse_ref, *, dil, sub_len, tq, sub):
    i = pl.program_id(2)
    nk = sub + 2 * ATTN_RADIUS
    qi = lax.broadcasted_iota(jnp.int32, (sub, nk), 0)
    kj = lax.broadcasted_iota(jnp.int32, (sub, nk), 1)
    band = jnp.abs(kj - ATTN_RADIUS - qi) <= ATTN_RADIUS
    scale = 1.0 / float(np.sqrt(HEAD_DIM))

    def rows(ref, r, n):
        return ref[0, pl.ds(r, n, stride=dil), :] if dil > 1 else ref[0, pl.ds(r, n), :]

    def ext(p_ref, m_ref, n_ref, r):
        parts = [rows(p_ref, r, ATTN_RADIUS), rows(m_ref, r, tq), rows(n_ref, r, ATTN_RADIUS)]
        return jnp.concatenate(parts, axis=0).astype(BF16)

    for r in range(dil):
        q = rows(q_ref, r, tq).astype(BF16)
        kext = ext(kp_ref, km_ref, kn_ref, r)
        vext = ext(vp_ref, vm_ref, vn_ref, r)
        for sb in range(tq // sub):
            r0 = sb * sub
            s = lax.dot_general(q[r0:r0 + sub], kext[r0:r0 + nk], (((1,), (1,)), ((), ())),
                                preferred_element_type=F32) * scale
            kpos = i * tq + (r0 - ATTN_RADIUS) + kj
            mask = band & (kpos >= 0) & (kpos < sub_len)
            s = jnp.where(mask, s, NEG_INF)
            m = jnp.max(s, axis=-1, keepdims=True)
            p = jnp.exp(s - m)
            den = jnp.sum(p, axis=-1, keepdims=True)
            o = jnp.dot(p.astype(BF16), vext[r0:r0 + nk], preferred_element_type=F32) / den
            lse = jnp.broadcast_to(m + jnp.log(den), (sub, LANES))
            if dil > 1:
                dst = pl.ds(r + r0 * dil, sub, stride=dil)
            else:
                dst = pl.ds(r0, sub)
            o_ref[dst, :] = o
            lse_ref[dst, :] = lse


def _attention_group(proj, g, dil, batch, seq, cfg):
    T = proj.shape[1]
    sub_len = seq // dil
    tt = min(cfg["attn_tokens"], seq)
    tq = tt // dil
    sub = min(cfg["attn_sub"], tq)
    halo = ATTN_RADIUS * dil
    assert tq % ATTN_RADIUS == 0 and tq % sub == 0 and seq % tt == 0
    n_t = seq // tt
    halo_per_tile = tt // halo
    halo_per_seq = seq // halo
    h0 = g * HEADS_PER_GROUP

    def main(cb):
        return pl.BlockSpec((1, tt, LANES), lambda b, h, i: (cb + h0 + h, b * n_t + i, 0))

    def prev(cb):
        return pl.BlockSpec((1, halo, LANES),
                            lambda b, h, i: (cb + h0 + h, b * halo_per_seq + jnp.maximum(i * halo_per_tile - 1, 0), 0))

    def nxt(cb):
        return pl.BlockSpec((1, halo, LANES),
                            lambda b, h, i: (cb + h0 + h,
                                             b * halo_per_seq + jnp.minimum((i + 1) * halo_per_tile, halo_per_seq - 1), 0))

    out_spec = pl.BlockSpec((tt, LANES), lambda b, h, i: (b * n_t + i, h))
    out_sds = jax.ShapeDtypeStruct((T, ATTN_OUT_W), F32)
    kern = functools.partial(_attn_kernel, dil=dil, sub_len=sub_len, tq=tq, sub=sub)
    return pl.pallas_call(
        kern,
        grid=(batch, HEADS_PER_GROUP, n_t),
        in_specs=[main(CB_Q), prev(CB_K), main(CB_K), nxt(CB_K), prev(CB_V), main(CB_V), nxt(CB_V)],
        out_specs=[out_spec, out_spec],
        out_shape=[out_sds, out_sds],
        compiler_params=_params("parallel", "parallel", "parallel"),
        name=f"attn_g{g}",
    )(proj, proj, proj, proj, proj, proj, proj)


def _merge_kernel(h_ref, g_ref, gr_ref, ga_ref, o0_ref, l0_ref, o1_ref, l1_ref, o2_ref, l2_ref,
                  wr_ref, wa_ref, out_ref):
    hg = [(h_ref[j] * jax.nn.gelu(g_ref[j])).astype(BF16) for j in range(RNN_BLOCKS)]
    y_rnn = jnp.dot(jnp.concatenate(hg, axis=-1), wr_ref[...], preferred_element_type=F32)
    pieces = []
    for h in range(HEADS_PER_GROUP):
        sl = slice(h * LANES, (h + 1) * LANES)
        ls = [l_ref[:, sl] for l_ref in (l0_ref, l1_ref, l2_ref)]
        os_ = [o_ref[:, sl] for o_ref in (o0_ref, o1_ref, o2_ref)]
        mx = jnp.maximum(jnp.maximum(ls[0], ls[1]), ls[2])
        es = [jnp.exp(l - mx) for l in ls]
        den = es[0] + es[1] + es[2]
        pieces.append(((es[0] * os_[0] + es[1] * os_[1] + es[2] * os_[2]) / den).astype(BF16))
    y_attn = jnp.dot(jnp.concatenate(pieces, axis=-1), wa_ref[...], preferred_element_type=F32)
    for j in range(RNN_BLOCKS):
        sl = slice(j * LANES, (j + 1) * LANES)
        mixed = jax.nn.sigmoid(gr_ref[j]) * y_rnn[:, sl] + jax.nn.sigmoid(ga_ref[j]) * y_attn[:, sl]
        out_ref[:, sl] = mixed.astype(BF16)


def _merge(proj, hrnn, attn_outs, w_rnn_proj, w_attn_proj, cfg):
    T = proj.shape[1]
    tm = cfg["merge_tm"]
    blk16 = lambda cb: pl.BlockSpec((RNN_BLOCKS, tm, LANES), lambda m: (cb // RNN_BLOCKS, m, 0))
    tok = lambda w: pl.BlockSpec((tm, w), lambda m: (m, 0))
    const = lambda a: pl.BlockSpec(a.shape, lambda m: (0, 0), pipeline_mode=pl.Buffered(1))
    flat = [a for pair in attn_outs for a in pair]
    return pl.pallas_call(
        _merge_kernel,
        grid=(T // tm,),
        in_specs=[blk16(0), blk16(CB_GRNN), blk16(CB_GATE_RNN), blk16(CB_GATE_ATTN)]
                 + [tok(ATTN_OUT_W)] * 6 + [const(w_rnn_proj), const(w_attn_proj)],
        out_specs=tok(D_MODEL),
        out_shape=jax.ShapeDtypeStruct((T, D_MODEL), BF16),
        compiler_params=_params("parallel"),
        name="merge",
    )(hrnn, proj, proj, proj, *flat, w_rnn_proj, w_attn_proj)


def _mlp_kernel(x_ref, mix_ref, wo_ref, g_ref, fg_ref, wu_ref, wd_ref, o_ref, h1_ref, xn_ref, acc_ref):
    f = pl.program_id(1)

    @pl.when(f == 0)
    def _():
        h1 = x_ref[...] + jnp.dot(mix_ref[...], wo_ref[...], preferred_element_type=F32)
        h1_ref[...] = h1
        xn_ref[...] = _rms(h1, g_ref[...]).astype(BF16)
        acc_ref[...] = jnp.zeros_like(acc_ref)

    u = jnp.dot(xn_ref[...], wu_ref[...], preferred_element_type=F32)
    act = jnp.square(jnp.maximum(u, 0.0)).astype(BF16)
    acc_ref[...] += jnp.dot(act, wd_ref[...], preferred_element_type=F32)

    @pl.when(f == pl.num_programs(1) - 1)
    def _():
        o_ref[...] = _rms(h1_ref[...] + acc_ref[...], fg_ref[...])


def _mlp(x2, mixed, w_out, mlp_g, final_g, w_up, w_down, cfg):
    T = x2.shape[0]
    tm, tf = cfg["mlp_tm"], cfg["mlp_tf"]
    tok = pl.BlockSpec((tm, D_MODEL), lambda m, f: (m, 0))
    vec = pl.BlockSpec((1, D_MODEL), lambda m, f: (0, 0))
    return pl.pallas_call(
        _mlp_kernel,
        grid=(T // tm, D_FF // tf),
        in_specs=[
            tok, tok,
            pl.BlockSpec((D_MODEL, D_MODEL), lambda m, f: (0, 0), pipeline_mode=pl.Buffered(1)),
            vec, vec,
            pl.BlockSpec((D_MODEL, tf), lambda m, f: (0, f)),
            pl.BlockSpec((tf, D_MODEL), lambda m, f: (f, 0)),
        ],
        out_specs=tok,
        out_shape=jax.ShapeDtypeStruct((T, D_MODEL), F32),
        scratch_shapes=[pltpu.VMEM((tm, D_MODEL), F32), pltpu.VMEM((tm, D_MODEL), BF16),
                        pltpu.VMEM((tm, D_MODEL), F32)],
        compiler_params=_params("parallel", "arbitrary"),
        name="mlp",
    )(x2, mixed, w_out, mlp_g, final_g, w_up, w_down)


def _rope_tables(seq):
    half = ROPE_DIM // 2
    inv_freq = ROPE_THETA ** (-jnp.arange(0, ROPE_DIM, 2, dtype=F32) / ROPE_DIM)
    ang = jnp.arange(seq, dtype=F32)[:, None] * inv_freq[None, :]
    cos, sin = jnp.cos(ang), jnp.sin(ang)
    rest = HEAD_DIM - ROPE_DIM
    cos_t = jnp.concatenate([cos, cos, jnp.ones((seq, rest), F32)], axis=-1)
    sin_t = jnp.concatenate([-sin, sin, jnp.zeros((seq, rest), F32)], axis=-1)
    assert cos_t.shape[1] == LANES and half * 2 == ROPE_DIM
    return cos_t, sin_t


def _layer(x, wts):
    batch, seq, _ = x.shape
    cfg = _tiles(seq)
    x2 = x.reshape(batch * seq, D_MODEL)
    cos_t, sin_t = _rope_tables(seq)
    proj = _inproj(x2, wts["mix_g"], wts["w_in"], cos_t, sin_t, seq, cfg)
    hrnn = _rnn(proj, wts["conv_w"], wts["conv_b"], wts["wg"], wts["bg"], wts["lam"], batch, seq, cfg)
    attn_outs = [_attention_group(proj, g, dil, batch, seq, cfg) for g, (_, dil) in enumerate(ATTN_PATTERNS)]
    mixed = _merge(proj, hrnn, attn_outs, wts["w_rnn_proj"], wts["w_attn_proj"], cfg)
    y = _mlp(x2, mixed, wts["w_out"], wts["mlp_g"], wts["final_g"], wts["w_up"], wts["w_down"], cfg)
    return y.reshape(batch, seq, D_MODEL)


def kernel(x_prompt, x_sample, mix_norm_g, w_in, conv_w, conv_b, lru_w_a, lru_b_a, lru_w_i, lru_b_i,
           lru_lambda, w_rnn_proj, w_attn_proj, w_out, mlp_norm_g, w_up, w_down, final_norm_g):
    assert mix_norm_g.shape[0] == 1, "single-layer stack"
    for window, dil in ATTN_PATTERNS:
        assert window // (2 * dil) == ATTN_RADIUS
    l = 0
    c0 = 2 * D_RNN
    c1 = c0 + 3 * ATTN_W
    w = w_in[l]
    w_in_p = jnp.concatenate([w[:, :c0], w[:, c1:], w[:, c0:c1]], axis=1).astype(BF16)
    nb = RNN_BLOCKS
    bw = D_RNN // nb
    wts = dict(
        mix_g=mix_norm_g[l][None, :],
        w_in=w_in_p,
        conv_w=conv_w[l],
        conv_b=conv_b[l][None, :],
        wg=jnp.concatenate([lru_w_a[l], lru_w_i[l]], axis=-1).astype(BF16),
        bg=jnp.concatenate([lru_b_a[l].reshape(2, nb, 1, bw), lru_b_i[l].reshape(2, nb, 1, bw)], axis=-1),
        lam=lru_lambda[l].reshape(2, nb, 1, bw),
        w_rnn_proj=w_rnn_proj[l].astype(BF16),
        w_attn_proj=w_attn_proj[l].astype(BF16),
        w_out=w_out[l].astype(BF16),
        mlp_g=mlp_norm_g[l][None, :],
        final_g=final_norm_g[None, :],
        w_up=w_up[l].astype(BF16),
        w_down=w_down[l].astype(BF16),
    )
    return (_layer(x_prompt, wts), _layer(x_sample, wts))
```

```python
import functools

import numpy as np
import jax
import jax.numpy as jnp
from jax import lax
from jax.experimental import pallas as pl
from jax.experimental.pallas import tpu as pltpu

F32 = jnp.float32
BF16 = jnp.bfloat16

LANES = 128
SUBLANES = 8
VMEM_LIMIT = 56 * 1024 * 1024

D_MODEL = 2048
D_RNN = D_MODEL
RNN_BLOCKS = 16
RNN_GROUPS = RNN_BLOCKS // SUBLANES
CONV_W = 4
LRU_C = 8.0
HEAD_DIM = 128
ATTN_PATTERNS = ((128, 1), (512, 4), (2048, 16))
HEADS_PER_GROUP = 4
N_HEADS = len(ATTN_PATTERNS) * HEADS_PER_GROUP
ATTN_W = N_HEADS * HEAD_DIM
ATTN_OUT_W = HEADS_PER_GROUP * HEAD_DIM
ATTN_RADIUS = 64
ROPE_DIM = HEAD_DIM // 4
ROPE_HALF = ROPE_DIM // 2
ROPE_THETA = 500000.0
D_FF = 4 * D_MODEL
EPS = 1e-6
NEG_INF = -1e30

CB_XRNN = 0
CB_GRNN = 16
CB_GATE_RNN = 32
CB_GATE_ATTN = 48
CB_Q = 64
CB_K = CB_Q + N_HEADS
CB_V = CB_K + N_HEADS
IN_BLOCKS = CB_V + N_HEADS
IN_COLS = IN_BLOCKS * LANES


def _tiles(seq):
    return dict(
        proj_tm=min(1024, seq), proj_tn=512,
        rnn_chunk=min(512, seq),
        attn_tokens=2048, attn_sub=128,
        merge_tm=min(256, seq),
        mlp_tm=min(512, seq), mlp_tf=512,
    )


def _params(*sem):
    return pltpu.CompilerParams(dimension_semantics=sem, vmem_limit_bytes=VMEM_LIMIT)


def _rms(xf, g):
    return xf * lax.rsqrt(jnp.mean(xf * xf, axis=-1, keepdims=True) + EPS) * g


def _sigmoid(z):
    return 0.5 * jnp.tanh(0.5 * z) + 0.5


def _inproj_kernel(x_ref, g_ref, w_ref, cos_ref, sin_ref, o_ref, xn_ref, *, rope_lo, rope_hi):
    n = pl.program_id(1)

    @pl.when(n == 0)
    def _():
        xn_ref[...] = _rms(x_ref[...], g_ref[...]).astype(BF16)

    acc = jnp.dot(xn_ref[...], w_ref[...], preferred_element_type=F32)
    nblk = o_ref.shape[0]
    is_rope = jnp.logical_and(n >= rope_lo, n < rope_hi)

    @pl.when(is_rope)
    def _():
        c = cos_ref[...]
        s = sin_ref[...]
        for j in range(nblk):
            t = acc[:, j * LANES:(j + 1) * LANES]
            o_ref[j] = t * c + pltpu.roll(t, LANES // 2, 1) * s

    @pl.when(jnp.logical_not(is_rope))
    def _():
        for j in range(nblk):
            o_ref[j] = acc[:, j * LANES:(j + 1) * LANES]


def _inproj(x2, g, w_in_p, cos_t, sin_t, seq, cfg):
    T = x2.shape[0]
    tm, tn = cfg["proj_tm"], cfg["proj_tn"]
    nb = tn // LANES
    tiles_per_seq = seq // tm
    kern = functools.partial(_inproj_kernel, rope_lo=CB_Q // nb, rope_hi=CB_V // nb)
    return pl.pallas_call(
        kern,
        grid=(T // tm, IN_COLS // tn),
        in_specs=[
            pl.BlockSpec((tm, D_MODEL), lambda m, n: (m, 0)),
            pl.BlockSpec((1, D_MODEL), lambda m, n: (0, 0)),
            pl.BlockSpec((D_MODEL, tn), lambda m, n: (0, n)),
            pl.BlockSpec((tm, LANES), lambda m, n: (m % tiles_per_seq, 0)),
            pl.BlockSpec((tm, LANES), lambda m, n: (m % tiles_per_seq, 0)),
        ],
        out_specs=pl.BlockSpec((nb, tm, LANES), lambda m, n: (n, m, 0)),
        out_shape=jax.ShapeDtypeStruct((IN_BLOCKS, T, LANES), F32),
        scratch_shapes=[pltpu.VMEM((tm, D_MODEL), BF16)],
        compiler_params=_params("parallel", "arbitrary"),
        name="inproj",
    )(x2, g, w_in_p, cos_t, sin_t)


def _rglru_kernel(x_ref, xp_ref, xn_ref, cw_ref, cb_ref, wg_ref, bg_ref, lam_ref, h_ref,
                  a_scr, u_scr, state_ref, *, chunk, n_chunks):
    d = pl.program_id(0)
    ci = pl.program_id(3)
    c_seq = jnp.where(d == 0, ci, n_chunks - 1 - ci)
    ext_rows = chunk + 2 * SUBLANES
    mid = slice(SUBLANES, SUBLANES + chunk)

    @pl.when(ci == 0)
    def _():
        state_ref[...] = jnp.zeros_like(state_ref)

    def gate_block(c, carry):
        x = x_ref[c]
        prev = jnp.where(c_seq == 0, 0.0, xp_ref[c])
        nxt = jnp.where(c_seq == n_chunks - 1, 0.0, xn_ref[c])
        ext = jnp.concatenate([prev, x, nxt], axis=0)
        cw = cw_ref[c]
        xc = (cb_ref[c] + pltpu.roll(ext, 2, 0)[mid] * cw[0:1] + pltpu.roll(ext, 1, 0)[mid] * cw[1:2]
              + x * cw[2:3] + pltpu.roll(ext, ext_rows - 1, 0)[mid] * cw[3:4])
        gz = jnp.dot(xc.astype(BF16), wg_ref[0, c], preferred_element_type=F32) + bg_ref[0, c]
        r = _sigmoid(gz[:, :LANES])
        i = _sigmoid(gz[:, LANES:])
        z = -lam_ref[0, c]
        softplus = jnp.maximum(z, 0.0) + jnp.log1p(jnp.exp(-jnp.abs(z)))
        log_a = -LRU_C * r * softplus
        a = jnp.exp(log_a)
        u = jnp.sqrt(-jnp.tanh(log_a) * (a * a + 1.0)) * (i * xc)
        a_scr[pl.ds(c, chunk, stride=SUBLANES), :] = a
        u_scr[pl.ds(c, chunk, stride=SUBLANES), :] = u
        return carry

    lax.fori_loop(0, SUBLANES, gate_block, 0)

    def step(t, h):
        r0 = pl.multiple_of(t * SUBLANES, SUBLANES)
        h = a_scr[pl.ds(r0, SUBLANES), :] * h + u_scr[pl.ds(r0, SUBLANES), :]
        h_ref[0, 0, pl.ds(r0, SUBLANES), :] = h
        return h

    @pl.when(d == 0)
    def _():
        state_ref[...] = lax.fori_loop(0, chunk, step, state_ref[...], unroll=8)

    @pl.when(d == 1)
    def _():
        state_ref[...] = lax.fori_loop(0, chunk, lambda k, h: step(chunk - 1 - k, h), state_ref[...], unroll=8)


def _rglru(proj, conv_w, conv_b, wg, bg, lam, batch, seq, cfg):
    T = proj.shape[1]
    chunk = cfg["rnn_chunk"]
    n_chunks = seq // chunk
    halo_per_chunk = chunk // SUBLANES
    halo_per_seq = seq // SUBLANES

    def pos(d, ci):
        return ci + d * (n_chunks - 1 - 2 * ci)

    per_group = lambda shape: pl.BlockSpec((SUBLANES,) + shape, lambda d, b, g, ci: (g,) + (0,) * len(shape))
    per_dir_group = lambda shape: pl.BlockSpec((1, SUBLANES) + shape,
                                               lambda d, b, g, ci: (d, g) + (0,) * len(shape))
    kern = functools.partial(_rglru_kernel, chunk=chunk, n_chunks=n_chunks)
    return pl.pallas_call(
        kern,
        grid=(2, batch, RNN_GROUPS, n_chunks),
        in_specs=[
            pl.BlockSpec((SUBLANES, chunk, LANES), lambda d, b, g, ci: (g, b * n_chunks + pos(d, ci), 0)),
            pl.BlockSpec((SUBLANES, SUBLANES, LANES),
                         lambda d, b, g, ci: (g, b * halo_per_seq + jnp.maximum(pos(d, ci) * halo_per_chunk - 1, 0), 0)),
            pl.BlockSpec((SUBLANES, SUBLANES, LANES),
                         lambda d, b, g, ci: (g, b * halo_per_seq
                                              + jnp.minimum((pos(d, ci) + 1) * halo_per_chunk, halo_per_seq - 1), 0)),
            per_group((CONV_W, LANES)),
            per_group((1, LANES)),
            per_dir_group((LANES, 2 * LANES)),
            per_dir_group((1, 2 * LANES)),
            per_dir_group((1, LANES)),
        ],
        out_specs=pl.BlockSpec((1, 1, chunk * SUBLANES, LANES),
                               lambda d, b, g, ci: (d, g, b * n_chunks + pos(d, ci), 0)),
        out_shape=jax.ShapeDtypeStruct((2, RNN_GROUPS, T * SUBLANES, LANES), F32),
        scratch_shapes=[pltpu.VMEM((chunk * SUBLANES, LANES), F32), pltpu.VMEM((chunk * SUBLANES, LANES), F32),
                        pltpu.VMEM((SUBLANES, LANES), F32)],
        compiler_params=_params("parallel", "parallel", "parallel", "arbitrary"),
        name="rglru",
    )(proj, proj, proj, conv_w, conv_b, wg, bg, lam)


def _attn_kernel(q_ref, kp_ref, km_ref, kn_ref, vp_ref, vm_ref, vn_ref, o_ref, lse_ref, *, dil, sub_len, tq, sub):
    i = pl.program_id(2)
    nk = sub + 2 * ATTN_RADIUS
    qi = lax.broadcasted_iota(jnp.int32, (sub, nk), 0)
    kj = lax.broadcasted_iota(jnp.int32, (sub, nk), 1)
    band = jnp.abs(kj - ATTN_RADIUS - qi) <= ATTN_RADIUS
    scale = 1.0 / float(np.sqrt(HEAD_DIM))

    def rows(ref, r, n):
        return ref[0, pl.ds(r, n, stride=dil), :] if dil > 1 else ref[0, pl.ds(r, n), :]

    def ext(p_ref, m_ref, n_ref, r):
        parts = [rows(p_ref, r, ATTN_RADIUS), rows(m_ref, r, tq), rows(n_ref, r, ATTN_RADIUS)]
        return jnp.concatenate(parts, axis=0).astype(BF16)

    for r in range(dil):
        q = rows(q_ref, r, tq).astype(BF16)
        kext = ext(kp_ref, km_ref, kn_ref, r)
        vext = ext(vp_ref, vm_ref, vn_ref, r)
        for sb in range(tq // sub):
            r0 = sb * sub
            s = lax.dot_general(q[r0:r0 + sub], kext[r0:r0 + nk], (((1,), (1,)), ((), ())),
                                preferred_element_type=F32) * scale
            kpos = i * tq + (r0 - ATTN_RADIUS) + kj
            mask = band & (kpos >= 0) & (kpos < sub_len)
            s = jnp.where(mask, s, NEG_INF)
            m = jnp.max(s, axis=-1, keepdims=True)
            p = jnp.exp(s - m)
            den = jnp.sum(p, axis=-1, keepdims=True)
            o = jnp.dot(p.astype(BF16), vext[r0:r0 + nk], preferred_element_type=F32) / den
            lse = jnp.broadcast_to(m + jnp.log(den), (sub, LANES))
            if dil > 1:
                dst = pl.ds(r + r0 * dil, sub, stride=dil)
            else:
                dst = pl.ds(r0, sub)
            o_ref[dst, :] = o
            lse_ref[dst, :] = lse


def _attention_group(proj, g, dil, batch, seq, cfg):
    T = proj.shape[1]
    sub_len = seq // dil
    tt = min(cfg["attn_tokens"], seq)
    tq = tt // dil
    sub = min(cfg["attn_sub"], tq)
    halo = ATTN_RADIUS * dil
    assert tq % ATTN_RADIUS == 0 and tq % sub == 0 and seq % tt == 0
    n_t = seq // tt
    halo_per_tile = tt // halo
    halo_per_seq = seq // halo
    h0 = g * HEADS_PER_GROUP

    def main(cb):
        return pl.BlockSpec((1, tt, LANES), lambda b, h, i: (cb + h0 + h, b * n_t + i, 0))

    def prev(cb):
        return pl.BlockSpec((1, halo, LANES),
                            lambda b, h, i: (cb + h0 + h, b * halo_per_seq + jnp.maximum(i * halo_per_tile - 1, 0), 0))

    def nxt(cb):
        return pl.BlockSpec((1, halo, LANES),
                            lambda b, h, i: (cb + h0 + h,
                                             b * halo_per_seq + jnp.minimum((i + 1) * halo_per_tile, halo_per_seq - 1), 0))

    out_spec = pl.BlockSpec((tt, LANES), lambda b, h, i: (b * n_t + i, h))
    out_sds = jax.ShapeDtypeStruct((T, ATTN_OUT_W), F32)
    kern = functools.partial(_attn_kernel, dil=dil, sub_len=sub_len, tq=tq, sub=sub)
    return pl.pallas_call(
        kern,
        grid=(batch, HEADS_PER_GROUP, n_t),
        in_specs=[main(CB_Q), prev(CB_K), main(CB_K), nxt(CB_K), prev(CB_V), main(CB_V), nxt(CB_V)],
        out_specs=[out_spec, out_spec],
        out_shape=[out_sds, out_sds],
        compiler_params=_params("parallel", "parallel", "parallel"),
        name=f"attn_g{g}",
    )(proj, proj, proj, proj, proj, proj, proj)


def _merge_kernel(hf0_ref, hf1_ref, hb0_ref, hb1_ref, g_ref, gr_ref, ga_ref,
                  o0_ref, l0_ref, o1_ref, l1_ref, o2_ref, l2_ref, wr_ref, wa_ref, out_ref):
    tm = out_ref.shape[0]
    hg = []
    for j in range(RNN_BLOCKS):
        grp, c = divmod(j, SUBLANES)
        hf_ref, hb_ref = ((hf0_ref, hb0_ref), (hf1_ref, hb1_ref))[grp]
        sel = pl.ds(c, tm, stride=SUBLANES)
        h = hf_ref[0, 0, sel, :] + hb_ref[0, 0, sel, :]
        hg.append((h * jax.nn.gelu(g_ref[j])).astype(BF16))
    y_rnn = jnp.dot(jnp.concatenate(hg, axis=-1), wr_ref[...], preferred_element_type=F32)
    pieces = []
    for h in range(HEADS_PER_GROUP):
        sl = slice(h * LANES, (h + 1) * LANES)
        ls = [l_ref[:, sl] for l_ref in (l0_ref, l1_ref, l2_ref)]
        os_ = [o_ref[:, sl] for o_ref in (o0_ref, o1_ref, o2_ref)]
        mx = jnp.maximum(jnp.maximum(ls[0], ls[1]), ls[2])
        es = [jnp.exp(l - mx) for l in ls]
        den = es[0] + es[1] + es[2]
        pieces.append(((es[0] * os_[0] + es[1] * os_[1] + es[2] * os_[2]) / den).astype(BF16))
    y_attn = jnp.dot(jnp.concatenate(pieces, axis=-1), wa_ref[...], preferred_element_type=F32)
    for j in range(RNN_BLOCKS):
        sl = slice(j * LANES, (j + 1) * LANES)
        mixed = _sigmoid(gr_ref[j]) * y_rnn[:, sl] + _sigmoid(ga_ref[j]) * y_attn[:, sl]
        out_ref[:, sl] = mixed.astype(BF16)


def _merge(proj, hrnn, attn_outs, w_rnn_proj, w_attn_proj, cfg):
    T = proj.shape[1]
    tm = cfg["merge_tm"]
    blk16 = lambda cb: pl.BlockSpec((RNN_BLOCKS, tm, LANES), lambda m: (cb // RNN_BLOCKS, m, 0))
    hblk = lambda d, grp: pl.BlockSpec((1, 1, tm * SUBLANES, LANES), lambda m: (d, grp, m, 0))
    tok = lambda w: pl.BlockSpec((tm, w), lambda m: (m, 0))
    const = lambda a: pl.BlockSpec(a.shape, lambda m: (0, 0), pipeline_mode=pl.Buffered(1))
    flat = [a for pair in attn_outs for a in pair]
    return pl.pallas_call(
        _merge_kernel,
        grid=(T // tm,),
        in_specs=[hblk(0, 0), hblk(0, 1), hblk(1, 0), hblk(1, 1),
                  blk16(CB_GRNN), blk16(CB_GATE_RNN), blk16(CB_GATE_ATTN)]
                 + [tok(ATTN_OUT_W)] * 6 + [const(w_rnn_proj), const(w_attn_proj)],
        out_specs=tok(D_MODEL),
        out_shape=jax.ShapeDtypeStruct((T, D_MODEL), BF16),
        compiler_params=_params("parallel"),
        name="merge",
    )(hrnn, hrnn, hrnn, hrnn, proj, proj, proj, *flat, w_rnn_proj, w_attn_proj)


def _mlp_kernel(x_ref, mix_ref, wo_ref, g_ref, fg_ref, wu_ref, wd_ref, o_ref, h1_ref, xn_ref, acc_ref):
    f = pl.program_id(1)

    @pl.when(f == 0)
    def _():
        h1 = x_ref[...] + jnp.dot(mix_ref[...], wo_ref[...], preferred_element_type=F32)
        h1_ref[...] = h1
        xn_ref[...] = _rms(h1, g_ref[...]).astype(BF16)
        acc_ref[...] = jnp.zeros_like(acc_ref)

    u = jnp.dot(xn_ref[...], wu_ref[...], preferred_element_type=F32)
    act = jnp.square(jnp.maximum(u, 0.0)).astype(BF16)
    acc_ref[...] += jnp.dot(act, wd_ref[...], preferred_element_type=F32)

    @pl.when(f == pl.num_programs(1) - 1)
    def _():
        o_ref[...] = _rms(h1_ref[...] + acc_ref[...], fg_ref[...])


def _mlp(x2, mixed, w_out, mlp_g, final_g, w_up, w_down, cfg):
    T = x2.shape[0]
    tm, tf = cfg["mlp_tm"], cfg["mlp_tf"]
    tok = pl.BlockSpec((tm, D_MODEL), lambda m, f: (m, 0))
    vec = pl.BlockSpec((1, D_MODEL), lambda m, f: (0, 0))
    return pl.pallas_call(
        _mlp_kernel,
        grid=(T // tm, D_FF // tf),
        in_specs=[
            tok, tok,
            pl.BlockSpec((D_MODEL, D_MODEL), lambda m, f: (0, 0), pipeline_mode=pl.Buffered(1)),
            vec, vec,
            pl.BlockSpec((D_MODEL, tf), lambda m, f: (0, f)),
            pl.BlockSpec((tf, D_MODEL), lambda m, f: (f, 0)),
        ],
        out_specs=tok,
        out_shape=jax.ShapeDtypeStruct((T, D_MODEL), F32),
        scratch_shapes=[pltpu.VMEM((tm, D_MODEL), F32), pltpu.VMEM((tm, D_MODEL), BF16),
                        pltpu.VMEM((tm, D_MODEL), F32)],
        compiler_params=_params("parallel", "arbitrary"),
        name="mlp",
    )(x2, mixed, w_out, mlp_g, final_g, w_up, w_down)


def _rope_lane_perm():
    dims = np.arange(HEAD_DIM)
    rest = dims[ROPE_DIM:]
    n_mid = LANES // 2 - ROPE_HALF
    return np.concatenate([dims[:ROPE_HALF], rest[:n_mid], dims[ROPE_HALF:ROPE_DIM], rest[n_mid:]])


def _rope_tables(seq):
    inv_freq = ROPE_THETA ** (-jnp.arange(0, ROPE_DIM, 2, dtype=F32) / ROPE_DIM)
    ang = jnp.arange(seq, dtype=F32)[:, None] * inv_freq[None, :]
    cos, sin = jnp.cos(ang), jnp.sin(ang)
    pad = LANES // 2 - ROPE_HALF
    ones, zeros = jnp.ones((seq, pad), F32), jnp.zeros((seq, pad), F32)
    cos_t = jnp.concatenate([cos, ones, cos, ones], axis=-1)
    sin_t = jnp.concatenate([-sin, zeros, sin, zeros], axis=-1)
    return cos_t, sin_t


def _layer(x, wts):
    batch, seq, _ = x.shape
    cfg = _tiles(seq)
    x2 = x.reshape(batch * seq, D_MODEL)
    cos_t, sin_t = _rope_tables(seq)
    proj = _inproj(x2, wts["mix_g"], wts["w_in"], cos_t, sin_t, seq, cfg)
    hrnn = _rglru(proj, wts["conv_w"], wts["conv_b"], wts["wg"], wts["bg"], wts["lam"], batch, seq, cfg)
    attn_outs = [_attention_group(proj, g, dil, batch, seq, cfg) for g, (_, dil) in enumerate(ATTN_PATTERNS)]
    mixed = _merge(proj, hrnn, attn_outs, wts["w_rnn_proj"], wts["w_attn_proj"], cfg)
    y = _mlp(x2, mixed, wts["w_out"], wts["mlp_g"], wts["final_g"], wts["w_up"], wts["w_down"], cfg)
    return y.reshape(batch, seq, D_MODEL)


def kernel(x_prompt, x_sample, mix_norm_g, w_in, conv_w, conv_b, lru_w_a, lru_b_a, lru_w_i, lru_b_i,
           lru_lambda, w_rnn_proj, w_attn_proj, w_out, mlp_norm_g, w_up, w_down, final_norm_g):
    assert mix_norm_g.shape[0] == 1, "single-layer stack"
    for window, dil in ATTN_PATTERNS:
        assert window // (2 * dil) == ATTN_RADIUS
    l = 0
    c0 = 2 * D_RNN
    c1 = c0 + 3 * ATTN_W
    w = w_in[l]
    perm = _rope_lane_perm()
    qk = w[:, c0:c0 + 2 * ATTN_W].reshape(D_MODEL, 2 * N_HEADS, HEAD_DIM)[:, :, perm].reshape(D_MODEL, 2 * ATTN_W)
    w_in_p = jnp.concatenate([w[:, :c0], w[:, c1:], qk, w[:, c0 + 2 * ATTN_W:c1]], axis=1).astype(BF16)
    nb = RNN_BLOCKS
    bw = D_RNN // nb
    wts = dict(
        mix_g=mix_norm_g[l][None, :],
        w_in=w_in_p,
        conv_w=conv_w[l].reshape(CONV_W, nb, bw).transpose(1, 0, 2),
        conv_b=conv_b[l].reshape(nb, 1, bw),
        wg=jnp.concatenate([lru_w_a[l], lru_w_i[l]], axis=-1).astype(BF16),
        bg=jnp.concatenate([lru_b_a[l].reshape(2, nb, 1, bw), lru_b_i[l].reshape(2, nb, 1, bw)], axis=-1),
        lam=lru_lambda[l].reshape(2, nb, 1, bw),
        w_rnn_proj=w_rnn_proj[l].astype(BF16),
        w_attn_proj=w_attn_proj[l].astype(BF16),
        w_out=w_out[l].astype(BF16),
        mlp_g=mlp_norm_g[l][None, :],
        final_g=final_norm_g[None, :],
        w_up=w_up[l].astype(BF16),
        w_down=w_down[l].astype(BF16),
    )
    return (_layer(x_prompt, wts), _layer(x_sample, wts))
```

```python
import functools

import numpy as np
import jax
import jax.numpy as jnp
from jax import lax
from jax.experimental import pallas as pl
from jax.experimental.pallas import tpu as pltpu

F32 = jnp.float32
BF16 = jnp.bfloat16

LANES = 128
SUBLANES = 8
VMEM_LIMIT = 56 * 1024 * 1024

D_MODEL = 2048
D_RNN = D_MODEL
RNN_BLOCKS = 16
RNN_GROUPS = RNN_BLOCKS // SUBLANES
CONV_W = 4
LRU_C = 8.0
HEAD_DIM = 128
ATTN_PATTERNS = ((128, 1), (512, 4), (2048, 16))
HEADS_PER_GROUP = 4
N_HEADS = len(ATTN_PATTERNS) * HEADS_PER_GROUP
ATTN_W = N_HEADS * HEAD_DIM
ATTN_OUT_W = HEADS_PER_GROUP * HEAD_DIM
ATTN_RADIUS = 64
ROPE_DIM = HEAD_DIM // 4
ROPE_HALF = ROPE_DIM // 2
ROPE_THETA = 500000.0
D_FF = 4 * D_MODEL
EPS = 1e-6
NEG_INF = -1e30

CB_XRNN = 0
CB_GRNN = 16
CB_GATE_RNN = 32
CB_GATE_ATTN = 48
CB_Q = 64
CB_K = CB_Q + N_HEADS
CB_V = CB_K + N_HEADS
IN_BLOCKS = CB_V + N_HEADS
IN_COLS = IN_BLOCKS * LANES


def _tiles(seq):
    return dict(
        proj_tm=min(512, seq), proj_tn=2560,
        rnn_chunk=min(512, seq),
        attn_tokens=min(2048, seq), attn_sub=128,
        merge_tm=min(256, seq),
        mlp_tm=min(512, seq), mlp_tf=1024,
    )


def _params(*sem):
    return pltpu.CompilerParams(dimension_semantics=sem, vmem_limit_bytes=VMEM_LIMIT)


def _rms(xf, g):
    return xf * lax.rsqrt(jnp.mean(xf * xf, axis=-1, keepdims=True) + EPS) * g


def _sigmoid(z):
    return 0.5 * jnp.tanh(0.5 * z) + 0.5


def _inproj_kernel(x_ref, g_ref, w_ref, cos_ref, sin_ref, o_ref, xn_ref):
    n = pl.program_id(1)

    @pl.when(n == 0)
    def _():
        xn_ref[...] = _rms(x_ref[...], g_ref[...]).astype(BF16)

    acc = jnp.dot(xn_ref[...], w_ref[...], preferred_element_type=F32)
    nblk = o_ref.shape[0]
    tiles = {}
    for tile in range(IN_BLOCKS // nblk):
        flags = tuple(CB_Q <= tile * nblk + j < CB_V for j in range(nblk))
        tiles.setdefault(flags, []).append(tile)

    for flags, members in tiles.items():
        cond = functools.reduce(jnp.logical_or, [n == t for t in members])

        @pl.when(cond)
        def _(flags=flags):
            for j in range(nblk):
                t = acc[:, j * LANES:(j + 1) * LANES]
                if flags[j]:
                    t = t * cos_ref[...] + pltpu.roll(t, LANES // 2, 1) * sin_ref[...]
                o_ref[j] = t


def _inproj(x2, g, w_in_p, cos_t, sin_t, seq, cfg):
    T = x2.shape[0]
    tm, tn = cfg["proj_tm"], cfg["proj_tn"]
    nb = tn // LANES
    assert IN_BLOCKS % nb == 0
    tiles_per_seq = seq // tm
    kern = _inproj_kernel
    return pl.pallas_call(
        kern,
        grid=(T // tm, IN_COLS // tn),
        in_specs=[
            pl.BlockSpec((tm, D_MODEL), lambda m, n: (m, 0)),
            pl.BlockSpec((1, D_MODEL), lambda m, n: (0, 0)),
            pl.BlockSpec((D_MODEL, tn), lambda m, n: (0, n)),
            pl.BlockSpec((tm, LANES), lambda m, n: (m % tiles_per_seq, 0)),
            pl.BlockSpec((tm, LANES), lambda m, n: (m % tiles_per_seq, 0)),
        ],
        out_specs=pl.BlockSpec((nb, tm, LANES), lambda m, n: (n, m, 0)),
        out_shape=jax.ShapeDtypeStruct((IN_BLOCKS, T, LANES), F32),
        scratch_shapes=[pltpu.VMEM((tm, D_MODEL), BF16)],
        compiler_params=_params("parallel", "arbitrary"),
        name="inproj",
    )(x2, g, w_in_p, cos_t, sin_t)


def _rglru_kernel(x_ref, xp_ref, xn_ref, cw_ref, cb_ref, wg_ref, bg_ref, lam_ref, h_ref,
                  a_scr, u_scr, state_ref, *, chunk, n_chunks):
    d = pl.program_id(0)
    ci = pl.program_id(3)
    c_seq = jnp.where(d == 0, ci, n_chunks - 1 - ci)
    ext_rows = chunk + 2 * SUBLANES
    mid = slice(SUBLANES, SUBLANES + chunk)

    @pl.when(ci == 0)
    def _():
        state_ref[...] = jnp.zeros_like(state_ref)

    def gate_block(c, carry):
        x = x_ref[c]
        prev = jnp.where(c_seq == 0, 0.0, xp_ref[c])
        nxt = jnp.where(c_seq == n_chunks - 1, 0.0, xn_ref[c])
        cw, cb, wg, bg, lam = cw_ref[c], cb_ref[c], wg_ref[0, c], bg_ref[0, c], lam_ref[0, c]
        ext = jnp.concatenate([prev, x, nxt], axis=0)
        xc = (cb + pltpu.roll(ext, 2, 0)[mid] * cw[0:1] + pltpu.roll(ext, 1, 0)[mid] * cw[1:2]
              + x * cw[2:3] + pltpu.roll(ext, ext_rows - 1, 0)[mid] * cw[3:4])
        gz = jnp.dot(xc.astype(BF16), wg, preferred_element_type=F32) + bg
        r = _sigmoid(gz[:, :LANES])
        i = _sigmoid(gz[:, LANES:])
        z = -lam
        softplus = jnp.maximum(z, 0.0) + jnp.log1p(jnp.exp(-jnp.abs(z)))
        log_a = -LRU_C * r * softplus
        a = jnp.exp(log_a)
        u = jnp.sqrt(-jnp.tanh(log_a) * (a * a + 1.0)) * (i * xc)
        a_scr[pl.ds(c, chunk, stride=SUBLANES), :] = a
        u_scr[pl.ds(c, chunk, stride=SUBLANES), :] = u
        return carry

    lax.fori_loop(0, SUBLANES, gate_block, 0)

    def step(j, h):
        t = j + d * (chunk - 1 - 2 * j)
        r0 = pl.multiple_of(t * SUBLANES, SUBLANES)
        h = a_scr[pl.ds(r0, SUBLANES), :] * h + u_scr[pl.ds(r0, SUBLANES), :]
        h_ref[0, 0, pl.ds(r0, SUBLANES), :] = h
        return h

    state_ref[...] = lax.fori_loop(0, chunk, step, state_ref[...], unroll=8)


def _rglru(proj, conv_w, conv_b, wg, bg, lam, batch, seq, cfg):
    T = proj.shape[1]
    chunk = cfg["rnn_chunk"]
    n_chunks = seq // chunk
    halo_per_chunk = chunk // SUBLANES
    halo_per_seq = seq // SUBLANES

    def pos(d, ci):
        return ci + d * (n_chunks - 1 - 2 * ci)

    per_group = lambda shape: pl.BlockSpec((SUBLANES,) + shape, lambda d, b, g, ci: (g,) + (0,) * len(shape))
    per_dir_group = lambda shape: pl.BlockSpec((1, SUBLANES) + shape,
                                               lambda d, b, g, ci: (d, g) + (0,) * len(shape))
    kern = functools.partial(_rglru_kernel, chunk=chunk, n_chunks=n_chunks)
    return pl.pallas_call(
        kern,
        grid=(2, batch, RNN_GROUPS, n_chunks),
        in_specs=[
            pl.BlockSpec((SUBLANES, chunk, LANES), lambda d, b, g, ci: (g, b * n_chunks + pos(d, ci), 0)),
            pl.BlockSpec((SUBLANES, SUBLANES, LANES),
                         lambda d, b, g, ci: (g, b * halo_per_seq + jnp.maximum(pos(d, ci) * halo_per_chunk - 1, 0), 0)),
            pl.BlockSpec((SUBLANES, SUBLANES, LANES),
                         lambda d, b, g, ci: (g, b * halo_per_seq
                                              + jnp.minimum((pos(d, ci) + 1) * halo_per_chunk, halo_per_seq - 1), 0)),
            per_group((CONV_W, LANES)),
            per_group((1, LANES)),
            per_dir_group((LANES, 2 * LANES)),
            per_dir_group((1, 2 * LANES)),
            per_dir_group((1, LANES)),
        ],
        out_specs=pl.BlockSpec((1, 1, chunk * SUBLANES, LANES),
                               lambda d, b, g, ci: (d, g, b * n_chunks + pos(d, ci), 0)),
        out_shape=jax.ShapeDtypeStruct((2, RNN_GROUPS, T * SUBLANES, LANES), F32),
        scratch_shapes=[pltpu.VMEM((chunk * SUBLANES, LANES), F32)] * 2 + [pltpu.VMEM((SUBLANES, LANES), F32)],
        compiler_params=_params("parallel", "parallel", "parallel", "arbitrary"),
        name="rglru",
    )(proj, proj, proj, conv_w, conv_b, wg, bg, lam)


ATTN_REFS_PER_GROUP = 7
ATTN_COMBINE_ROWS = 256


def _attn_kernel(*refs, seq, tt, sub_max):
    n_groups = len(ATTN_PATTERNS)
    in_refs = refs[:ATTN_REFS_PER_GROUP * n_groups]
    out_ref, o_scr, l_scr = refs[ATTN_REFS_PER_GROUP * n_groups:]
    i = pl.program_id(2)
    scale = 1.0 / float(np.sqrt(HEAD_DIM))

    for g, (_, dil) in enumerate(ATTN_PATTERNS):
        q_ref, kp_ref, km_ref, kn_ref, vp_ref, vm_ref, vn_ref = in_refs[ATTN_REFS_PER_GROUP * g:ATTN_REFS_PER_GROUP * (g + 1)]
        tq = tt // dil
        sub = min(sub_max, tq)
        sub_len = seq // dil
        nk = sub + 2 * ATTN_RADIUS
        qi = lax.broadcasted_iota(jnp.int32, (sub, nk), 0)
        kj = lax.broadcasted_iota(jnp.int32, (sub, nk), 1)
        band = jnp.abs(kj - ATTN_RADIUS - qi) <= ATTN_RADIUS

        def rows(ref, r, n, dil=dil):
            return ref[0, pl.ds(r, n, stride=dil), :] if dil > 1 else ref[0, pl.ds(r, n), :]

        def ext(p_ref, m_ref, n_ref, r, tq=tq, rows=rows):
            parts = [rows(p_ref, r, ATTN_RADIUS), rows(m_ref, r, tq), rows(n_ref, r, ATTN_RADIUS)]
            return jnp.concatenate(parts, axis=0).astype(BF16)

        for r in range(dil):
            q = rows(q_ref, r, tq).astype(BF16)
            kext = ext(kp_ref, km_ref, kn_ref, r)
            vext = ext(vp_ref, vm_ref, vn_ref, r)
            for sb in range(tq // sub):
                r0 = sb * sub
                s = lax.dot_general(q[r0:r0 + sub], kext[r0:r0 + nk], (((1,), (1,)), ((), ())),
                                    preferred_element_type=F32) * scale
                kpos = i * tq + (r0 - ATTN_RADIUS) + kj
                mask = band & (kpos >= 0) & (kpos < sub_len)
                s = jnp.where(mask, s, NEG_INF)
                m = jnp.max(s, axis=-1, keepdims=True)
                p = jnp.exp(s - m)
                den = jnp.sum(p, axis=-1, keepdims=True)
                o = jnp.dot(p.astype(BF16), vext[r0:r0 + nk], preferred_element_type=F32) / den
                lse = jnp.broadcast_to(m + jnp.log(den), (sub, LANES))
                dst = pl.ds(r + r0 * dil, sub, stride=dil) if dil > 1 else pl.ds(r0, sub)
                o_scr[g, dst, :] = o
                l_scr[g, dst, :] = lse

    rows_c = min(ATTN_COMBINE_ROWS, tt)

    def combine(k, carry):
        sl = pl.ds(pl.multiple_of(k * rows_c, rows_c), rows_c)
        ls = [l_scr[g, sl, :] for g in range(n_groups)]
        mx = functools.reduce(jnp.maximum, ls)
        es = [jnp.exp(l - mx) for l in ls]
        num = sum(e * o_scr[g, sl, :] for g, e in enumerate(es))
        out_ref[sl, :] = (num / sum(es)).astype(BF16)
        return carry

    lax.fori_loop(0, tt // rows_c, combine, 0)


def _attention(proj, batch, seq, cfg):
    T = proj.shape[1]
    tt = cfg["attn_tokens"]
    n_t = seq // tt
    assert seq % tt == 0
    in_specs = []
    for g, (_, dil) in enumerate(ATTN_PATTERNS):
        halo = ATTN_RADIUS * dil
        assert tt % halo == 0 and (tt // dil) % min(cfg["attn_sub"], tt // dil) == 0
        halo_per_tile = tt // halo
        halo_per_seq = seq // halo
        h0 = g * HEADS_PER_GROUP

        def main(cb, h0=h0):
            return pl.BlockSpec((1, tt, LANES), lambda b, h, i: (cb + h0 + h, b * n_t + i, 0))

        def prev(cb, h0=h0, halo=halo, hpt=halo_per_tile, hps=halo_per_seq):
            return pl.BlockSpec((1, halo, LANES),
                                lambda b, h, i: (cb + h0 + h, b * hps + jnp.maximum(i * hpt - 1, 0), 0))

        def nxt(cb, h0=h0, halo=halo, hpt=halo_per_tile, hps=halo_per_seq):
            return pl.BlockSpec((1, halo, LANES),
                                lambda b, h, i: (cb + h0 + h, b * hps + jnp.minimum((i + 1) * hpt, hps - 1), 0))

        in_specs += [main(CB_Q), prev(CB_K), main(CB_K), nxt(CB_K), prev(CB_V), main(CB_V), nxt(CB_V)]
    n_groups = len(ATTN_PATTERNS)
    kern = functools.partial(_attn_kernel, seq=seq, tt=tt, sub_max=cfg["attn_sub"])
    return pl.pallas_call(
        kern,
        grid=(batch, HEADS_PER_GROUP, n_t),
        in_specs=in_specs,
        out_specs=pl.BlockSpec((tt, LANES), lambda b, h, i: (b * n_t + i, h)),
        out_shape=jax.ShapeDtypeStruct((T, ATTN_OUT_W), BF16),
        scratch_shapes=[pltpu.VMEM((n_groups, tt, LANES), F32), pltpu.VMEM((n_groups, tt, LANES), F32)],
        compiler_params=_params("parallel", "parallel", "parallel"),
        name="attn",
    )(*([proj] * len(in_specs)))


def _merge_kernel(hf0_ref, hf1_ref, hb0_ref, hb1_ref, g_ref, gr_ref, ga_ref, attn_ref, wr_ref, wa_ref, out_ref):
    tm = out_ref.shape[0]
    hg = []
    for j in range(RNN_BLOCKS):
        grp, c = divmod(j, SUBLANES)
        hf_ref, hb_ref = ((hf0_ref, hb0_ref), (hf1_ref, hb1_ref))[grp]
        sel = pl.ds(c, tm, stride=SUBLANES)
        h = hf_ref[0, 0, sel, :] + hb_ref[0, 0, sel, :]
        hg.append((h * jax.nn.gelu(g_ref[j])).astype(BF16))
    y_rnn = jnp.dot(jnp.concatenate(hg, axis=-1), wr_ref[...], preferred_element_type=F32)
    y_attn = jnp.dot(attn_ref[...], wa_ref[...], preferred_element_type=F32)
    for j in range(RNN_BLOCKS):
        sl = slice(j * LANES, (j + 1) * LANES)
        mixed = _sigmoid(gr_ref[j]) * y_rnn[:, sl] + _sigmoid(ga_ref[j]) * y_attn[:, sl]
        out_ref[:, sl] = mixed.astype(BF16)


def _merge(proj, hrnn, attn, w_rnn_proj, w_attn_proj, cfg):
    T = proj.shape[1]
    tm = cfg["merge_tm"]
    blk16 = lambda cb: pl.BlockSpec((RNN_BLOCKS, tm, LANES), lambda m: (cb // RNN_BLOCKS, m, 0))
    hblk = lambda d, grp: pl.BlockSpec((1, 1, tm * SUBLANES, LANES), lambda m: (d, grp, m, 0))
    tok = lambda w: pl.BlockSpec((tm, w), lambda m: (m, 0))
    const = lambda a: pl.BlockSpec(a.shape, lambda m: (0, 0), pipeline_mode=pl.Buffered(1))
    return pl.pallas_call(
        _merge_kernel,
        grid=(T // tm,),
        in_specs=[hblk(0, 0), hblk(0, 1), hblk(1, 0), hblk(1, 1),
                  blk16(CB_GRNN), blk16(CB_GATE_RNN), blk16(CB_GATE_ATTN),
                  tok(ATTN_OUT_W), const(w_rnn_proj), const(w_attn_proj)],
        out_specs=tok(D_MODEL),
        out_shape=jax.ShapeDtypeStruct((T, D_MODEL), BF16),
        compiler_params=_params("parallel"),
        name="merge",
    )(hrnn, hrnn, hrnn, hrnn, proj, proj, proj, attn, w_rnn_proj, w_attn_proj)


def _mlp_kernel(x_ref, mix_ref, wo_ref, g_ref, fg_ref, wu_ref, wd_ref, o_ref, xn_ref, acc_ref):
    f = pl.program_id(1)

    @pl.when(f == 0)
    def _():
        h1 = x_ref[...] + jnp.dot(mix_ref[...], wo_ref[...], preferred_element_type=F32)
        xn_ref[...] = _rms(h1, g_ref[...]).astype(BF16)
        acc_ref[...] = h1

    u = jnp.dot(xn_ref[...], wu_ref[...], preferred_element_type=F32)
    act = jnp.square(jnp.maximum(u, 0.0)).astype(BF16)
    acc_ref[...] += jnp.dot(act, wd_ref[...], preferred_element_type=F32)

    @pl.when(f == pl.num_programs(1) - 1)
    def _():
        o_ref[...] = _rms(acc_ref[...], fg_ref[...])


def _mlp(x2, mixed, w_out, mlp_g, final_g, w_up, w_down, cfg):
    T = x2.shape[0]
    tm, tf = cfg["mlp_tm"], cfg["mlp_tf"]
    tok = pl.BlockSpec((tm, D_MODEL), lambda m, f: (m, 0))
    vec = pl.BlockSpec((1, D_MODEL), lambda m, f: (0, 0))
    return pl.pallas_call(
        _mlp_kernel,
        grid=(T // tm, D_FF // tf),
        in_specs=[
            tok, tok,
            pl.BlockSpec((D_MODEL, D_MODEL), lambda m, f: (0, 0), pipeline_mode=pl.Buffered(1)),
            vec, vec,
            pl.BlockSpec((D_MODEL, tf), lambda m, f: (0, f)),
            pl.BlockSpec((tf, D_MODEL), lambda m, f: (f, 0)),
        ],
        out_specs=tok,
        out_shape=jax.ShapeDtypeStruct((T, D_MODEL), F32),
        scratch_shapes=[pltpu.VMEM((tm, D_MODEL), BF16), pltpu.VMEM((tm, D_MODEL), F32)],
        compiler_params=_params("parallel", "arbitrary"),
        name="mlp",
    )(x2, mixed, w_out, mlp_g, final_g, w_up, w_down)


def _rope_lane_perm():
    dims = np.arange(HEAD_DIM)
    rest = dims[ROPE_DIM:]
    n_mid = LANES // 2 - ROPE_HALF
    return np.concatenate([dims[:ROPE_HALF], rest[:n_mid], dims[ROPE_HALF:ROPE_DIM], rest[n_mid:]])


def _rope_tables(seq):
    inv_freq = ROPE_THETA ** (-jnp.arange(0, ROPE_DIM, 2, dtype=F32) / ROPE_DIM)
    ang = jnp.arange(seq, dtype=F32)[:, None] * inv_freq[None, :]
    cos, sin = jnp.cos(ang), jnp.sin(ang)
    pad = LANES // 2 - ROPE_HALF
    ones, zeros = jnp.ones((seq, pad), F32), jnp.zeros((seq, pad), F32)
    cos_t = jnp.concatenate([cos, ones, cos, ones], axis=-1)
    sin_t = jnp.concatenate([-sin, zeros, sin, zeros], axis=-1)
    return cos_t, sin_t


def _layer(x, wts):
    batch, seq, _ = x.shape
    cfg = _tiles(seq)
    x2 = x.reshape(batch * seq, D_MODEL)
    cos_t, sin_t = _rope_tables(seq)
    proj = _inproj(x2, wts["mix_g"], wts["w_in"], cos_t, sin_t, seq, cfg)
    hrnn = _rglru(proj, wts["conv_w"], wts["conv_b"], wts["wg"], wts["bg"], wts["lam"], batch, seq, cfg)
    attn = _attention(proj, batch, seq, cfg)
    mixed = _merge(proj, hrnn, attn, wts["w_rnn_proj"], wts["w_attn_proj"], cfg)
    y = _mlp(x2, mixed, wts["w_out"], wts["mlp_g"], wts["final_g"], wts["w_up"], wts["w_down"], cfg)
    return y.reshape(batch, seq, D_MODEL)


def kernel(x_prompt, x_sample, mix_norm_g, w_in, conv_w, conv_b, lru_w_a, lru_b_a, lru_w_i, lru_b_i,
           lru_lambda, w_rnn_proj, w_attn_proj, w_out, mlp_norm_g, w_up, w_down, final_norm_g):
    assert mix_norm_g.shape[0] == 1, "single-layer stack"
    for window, dil in ATTN_PATTERNS:
        assert window // (2 * dil) == ATTN_RADIUS
    l = 0
    c0 = 2 * D_RNN
    c1 = c0 + 3 * ATTN_W
    w = w_in[l]
    perm = _rope_lane_perm()
    qk = w[:, c0:c0 + 2 * ATTN_W].reshape(D_MODEL, 2 * N_HEADS, HEAD_DIM)[:, :, perm].reshape(D_MODEL, 2 * ATTN_W)
    w_in_p = jnp.concatenate([w[:, :c0], w[:, c1:], qk, w[:, c0 + 2 * ATTN_W:c1]], axis=1).astype(BF16)
    nb = RNN_BLOCKS
    bw = D_RNN // nb
    wts = dict(
        mix_g=mix_norm_g[l][None, :],
        w_in=w_in_p,
        conv_w=conv_w[l].reshape(CONV_W, nb, bw).transpose(1, 0, 2),
        conv_b=conv_b[l].reshape(nb, 1, bw),
        wg=jnp.concatenate([lru_w_a[l], lru_w_i[l]], axis=-1).astype(BF16),
        bg=jnp.concatenate([lru_b_a[l].reshape(2, nb, 1, bw), lru_b_i[l].reshape(2, nb, 1, bw)], axis=-1),
        lam=lru_lambda[l].reshape(2, nb, 1, bw),
        w_rnn_proj=w_rnn_proj[l].astype(BF16),
        w_attn_proj=w_attn_proj[l].astype(BF16),
        w_out=w_out[l].astype(BF16),
        mlp_g=mlp_norm_g[l][None, :],
        final_g=final_norm_g[None, :],
        w_up=w_up[l].astype(BF16),
        w_down=w_down[l].astype(BF16),
    )
    return (_layer(x_prompt, wts), _layer(x_sample, wts))
```

```python
import functools

import numpy as np
import jax
import jax.numpy as jnp
from jax import lax
from jax.experimental import pallas as pl
from jax.experimental.pallas import tpu as pltpu

F32 = jnp.float32
BF16 = jnp.bfloat16

LANES = 128
SUBLANES = 8
VMEM_LIMIT = 56 * 1024 * 1024

D_MODEL = 2048
D_RNN = D_MODEL
RNN_BLOCKS = 16
RNN_GROUPS = RNN_BLOCKS // SUBLANES
CONV_W = 4
LRU_C = 8.0
HEAD_DIM = 128
ATTN_PATTERNS = ((128, 1), (512, 4), (2048, 16))
HEADS_PER_GROUP = 4
N_HEADS = len(ATTN_PATTERNS) * HEADS_PER_GROUP
ATTN_W = N_HEADS * HEAD_DIM
ATTN_OUT_W = HEADS_PER_GROUP * HEAD_DIM
ATTN_RADIUS = 64
ROPE_DIM = HEAD_DIM // 4
ROPE_HALF = ROPE_DIM // 2
ROPE_THETA = 500000.0
D_FF = 4 * D_MODEL
EPS = 1e-6
NEG_INF = -1e30

CB_XRNN = 0
CB_GRNN = 16
CB_GATE_RNN = 32
CB_GATE_ATTN = 48
CB_Q = 64
CB_K = CB_Q + N_HEADS
CB_V = CB_K + N_HEADS
IN_BLOCKS = CB_V + N_HEADS
IN_COLS = IN_BLOCKS * LANES


def _tiles(seq):
    return dict(
        proj_tm=min(1024, seq), proj_tn=1280,
        rnn_chunk=min(512, seq),
        attn_tokens=min(2048, seq), attn_sub=128,
        merge_tm=min(256, seq),
        mlp_tm=min(512, seq), mlp_tf=1024,
    )


def _params(*sem):
    return pltpu.CompilerParams(dimension_semantics=sem, vmem_limit_bytes=VMEM_LIMIT)


def _rms(xf, g):
    return xf * lax.rsqrt(jnp.mean(xf * xf, axis=-1, keepdims=True) + EPS) * g


def _sigmoid(z):
    return 0.5 * jnp.tanh(0.5 * z) + 0.5


def _inproj_kernel(x_ref, g_ref, w_ref, cos_ref, sin_ref, o_ref, xn_ref):
    n = pl.program_id(1)

    @pl.when(n == 0)
    def _():
        xn_ref[...] = _rms(x_ref[...], g_ref[...]).astype(BF16)

    nblk = o_ref.shape[0]
    tiles = {}
    for tile in range(IN_BLOCKS // nblk):
        flags = tuple(CB_Q <= tile * nblk + j < CB_V for j in range(nblk))
        tiles.setdefault(flags, []).append(tile)

    for flags, members in tiles.items():
        cond = functools.reduce(jnp.logical_or, [n == t for t in members])

        @pl.when(cond)
        def _(flags=flags):
            for j2 in range(nblk // 2):
                cols = slice(2 * j2 * LANES, 2 * (j2 + 1) * LANES)
                r = jnp.dot(xn_ref[...], w_ref[:, cols], preferred_element_type=F32)
                for half in range(2):
                    j = 2 * j2 + half
                    t = r[:, half * LANES:(half + 1) * LANES]
                    if flags[j]:
                        t = t * cos_ref[...] + pltpu.roll(t, LANES // 2, 1) * sin_ref[...]
                    o_ref[j] = t


def _inproj(x2, g, w_in_p, cos_t, sin_t, seq, cfg):
    T = x2.shape[0]
    tm, tn = cfg["proj_tm"], cfg["proj_tn"]
    nb = tn // LANES
    assert IN_BLOCKS % nb == 0 and nb % 2 == 0
    tiles_per_seq = seq // tm
    kern = _inproj_kernel
    return pl.pallas_call(
        kern,
        grid=(T // tm, IN_COLS // tn),
        in_specs=[
            pl.BlockSpec((tm, D_MODEL), lambda m, n: (m, 0)),
            pl.BlockSpec((1, D_MODEL), lambda m, n: (0, 0)),
            pl.BlockSpec((D_MODEL, tn), lambda m, n: (0, n)),
            pl.BlockSpec((tm, LANES), lambda m, n: (m % tiles_per_seq, 0)),
            pl.BlockSpec((tm, LANES), lambda m, n: (m % tiles_per_seq, 0)),
        ],
        out_specs=pl.BlockSpec((nb, tm, LANES), lambda m, n: (n, m, 0)),
        out_shape=jax.ShapeDtypeStruct((IN_BLOCKS, T, LANES), F32),
        scratch_shapes=[pltpu.VMEM((tm, D_MODEL), BF16)],
        compiler_params=_params("parallel", "arbitrary"),
        name="inproj",
    )(x2, g, w_in_p, cos_t, sin_t)


def _rglru_kernel(x_ref, xp_ref, xn_ref, cw_ref, cb_ref, wg_ref, bg_ref, lam_ref, h_ref,
                  a_scr, u_scr, state_ref, *, chunk, n_chunks):
    d = pl.program_id(0)
    ci = pl.program_id(3)
    c_seq = jnp.where(d == 0, ci, n_chunks - 1 - ci)
    ext_rows = chunk + 2 * SUBLANES
    mid = slice(SUBLANES, SUBLANES + chunk)

    @pl.when(ci == 0)
    def _():
        state_ref[...] = jnp.zeros_like(state_ref)

    def gate_block(c, carry):
        x = x_ref[c]
        prev = jnp.where(c_seq == 0, 0.0, xp_ref[c])
        nxt = jnp.where(c_seq == n_chunks - 1, 0.0, xn_ref[c])
        cw, cb, wg, bg, lam = cw_ref[c], cb_ref[c], wg_ref[0, c], bg_ref[0, c], lam_ref[0, c]
        ext = jnp.concatenate([prev, x, nxt], axis=0)
        xc = (cb + pltpu.roll(ext, 2, 0)[mid] * cw[0:1] + pltpu.roll(ext, 1, 0)[mid] * cw[1:2]
              + x * cw[2:3] + pltpu.roll(ext, ext_rows - 1, 0)[mid] * cw[3:4])
        tz = jnp.tanh(jnp.dot(xc.astype(BF16), wg, preferred_element_type=F32) + bg)
        i = 0.5 * tz[:, LANES:] + 0.5
        z = -lam
        softplus = jnp.maximum(z, 0.0) + jnp.log1p(jnp.exp(-jnp.abs(z)))
        half_rate = (-0.5 * LRU_C) * softplus
        log_a = half_rate * tz[:, :LANES] + half_rate
        a = jnp.exp(log_a)
        u = jnp.sqrt(-jnp.tanh(log_a) * (a * a + 1.0)) * (i * xc)
        a_scr[pl.ds(c, chunk, stride=SUBLANES), :] = a
        u_scr[pl.ds(c, chunk, stride=SUBLANES), :] = u
        return carry

    lax.fori_loop(0, SUBLANES, gate_block, 0)

    def step(t, h):
        r0 = pl.multiple_of(t * SUBLANES, SUBLANES)
        h = a_scr[pl.ds(r0, SUBLANES), :] * h + u_scr[pl.ds(r0, SUBLANES), :]
        h_ref[0, 0, pl.ds(r0, SUBLANES), :] = h
        return h

    @pl.when(d == 0)
    def _():
        state_ref[...] = lax.fori_loop(0, chunk, step, state_ref[...], unroll=8)

    @pl.when(d == 1)
    def _():
        state_ref[...] = lax.fori_loop(0, chunk, lambda k, h: step(chunk - 1 - k, h), state_ref[...], unroll=8)


def _rglru(proj, conv_w, conv_b, wg, bg, lam, batch, seq, cfg):
    T = proj.shape[1]
    chunk = cfg["rnn_chunk"]
    n_chunks = seq // chunk
    halo_per_chunk = chunk // SUBLANES
    halo_per_seq = seq // SUBLANES

    def pos(d, ci):
        return ci + d * (n_chunks - 1 - 2 * ci)

    per_group = lambda shape: pl.BlockSpec((SUBLANES,) + shape, lambda d, b, g, ci: (g,) + (0,) * len(shape))
    per_dir_group = lambda shape: pl.BlockSpec((1, SUBLANES) + shape,
                                               lambda d, b, g, ci: (d, g) + (0,) * len(shape))
    kern = functools.partial(_rglru_kernel, chunk=chunk, n_chunks=n_chunks)
    return pl.pallas_call(
        kern,
        grid=(2, batch, RNN_GROUPS, n_chunks),
        in_specs=[
            pl.BlockSpec((SUBLANES, chunk, LANES), lambda d, b, g, ci: (g, b * n_chunks + pos(d, ci), 0)),
            pl.BlockSpec((SUBLANES, SUBLANES, LANES),
                         lambda d, b, g, ci: (g, b * halo_per_seq + jnp.maximum(pos(d, ci) * halo_per_chunk - 1, 0), 0)),
            pl.BlockSpec((SUBLANES, SUBLANES, LANES),
                         lambda d, b, g, ci: (g, b * halo_per_seq
                                              + jnp.minimum((pos(d, ci) + 1) * halo_per_chunk, halo_per_seq - 1), 0)),
            per_group((CONV_W, LANES)),
            per_group((1, LANES)),
            per_dir_group((LANES, 2 * LANES)),
            per_dir_group((1, 2 * LANES)),
            per_dir_group((1, LANES)),
        ],
        out_specs=pl.BlockSpec((1, 1, chunk * SUBLANES, LANES),
                               lambda d, b, g, ci: (d, g, b * n_chunks + pos(d, ci), 0)),
        out_shape=jax.ShapeDtypeStruct((2, RNN_GROUPS, T * SUBLANES, LANES), F32),
        scratch_shapes=[pltpu.VMEM((chunk * SUBLANES, LANES), F32)] * 2 + [pltpu.VMEM((SUBLANES, LANES), F32)],
        compiler_params=_params("parallel", "parallel", "parallel", "arbitrary"),
        name="rglru",
    )(proj, proj, proj, conv_w, conv_b, wg, bg, lam)


ATTN_REFS_PER_GROUP = 7
ATTN_COMBINE_ROWS = 256


def _attn_kernel(*refs, seq, tt, sub_max):
    n_groups = len(ATTN_PATTERNS)
    in_refs = refs[:ATTN_REFS_PER_GROUP * n_groups]
    out_ref, o_scr, l_scr = refs[ATTN_REFS_PER_GROUP * n_groups:]
    i = pl.program_id(2)
    scale = 1.0 / float(np.sqrt(HEAD_DIM))

    for g, (_, dil) in enumerate(ATTN_PATTERNS):
        q_ref, kp_ref, km_ref, kn_ref, vp_ref, vm_ref, vn_ref = in_refs[ATTN_REFS_PER_GROUP * g:ATTN_REFS_PER_GROUP * (g + 1)]
        tq = tt // dil
        sub = min(sub_max, tq)
        sub_len = seq // dil
        nk = sub + 2 * ATTN_RADIUS
        qi = lax.broadcasted_iota(jnp.int32, (sub, nk), 0)
        kj = lax.broadcasted_iota(jnp.int32, (sub, nk), 1)
        band = jnp.abs(kj - ATTN_RADIUS - qi) <= ATTN_RADIUS

        def rows(ref, r, n, dil=dil):
            return ref[0, pl.ds(r, n, stride=dil), :] if dil > 1 else ref[0, pl.ds(r, n), :]

        def ext(p_ref, m_ref, n_ref, r, tq=tq, rows=rows):
            parts = [rows(p_ref, r, ATTN_RADIUS), rows(m_ref, r, tq), rows(n_ref, r, ATTN_RADIUS)]
            return jnp.concatenate(parts, axis=0).astype(BF16)

        for r in range(dil):
            q = rows(q_ref, r, tq).astype(BF16)
            kext = ext(kp_ref, km_ref, kn_ref, r)
            vext = ext(vp_ref, vm_ref, vn_ref, r)
            for sb in range(tq // sub):
                r0 = sb * sub
                s = lax.dot_general(q[r0:r0 + sub], kext[r0:r0 + nk], (((1,), (1,)), ((), ())),
                                    preferred_element_type=F32) * scale
                kpos = i * tq + (r0 - ATTN_RADIUS) + kj
                mask = band & (kpos >= 0) & (kpos < sub_len)
                s = jnp.where(mask, s, NEG_INF)
                m = jnp.max(s, axis=-1, keepdims=True)
                p = jnp.exp(s - m)
                den = jnp.sum(p, axis=-1, keepdims=True)
                o = jnp.dot(p.astype(BF16), vext[r0:r0 + nk], preferred_element_type=F32) / den
                lse = jnp.broadcast_to(m + jnp.log(den), (sub, LANES))
                dst = pl.ds(r + r0 * dil, sub, stride=dil) if dil > 1 else pl.ds(r0, sub)
                o_scr[g, dst, :] = o
                l_scr[g, dst, :] = lse

    rows_c = min(ATTN_COMBINE_ROWS, tt)

    def combine(k, carry):
        sl = pl.ds(pl.multiple_of(k * rows_c, rows_c), rows_c)
        ls = [l_scr[g, sl, :] for g in range(n_groups)]
        mx = functools.reduce(jnp.maximum, ls)
        es = [jnp.exp(l - mx) for l in ls]
        num = sum(e * o_scr[g, sl, :] for g, e in enumerate(es))
        out_ref[sl, :] = (num / sum(es)).astype(BF16)
        return carry

    lax.fori_loop(0, tt // rows_c, combine, 0)


def _attention(proj, batch, seq, cfg):
    T = proj.shape[1]
    tt = cfg["attn_tokens"]
    n_t = seq // tt
    assert seq % tt == 0
    in_specs = []
    for g, (_, dil) in enumerate(ATTN_PATTERNS):
        halo = ATTN_RADIUS * dil
        assert tt % halo == 0 and (tt // dil) % min(cfg["attn_sub"], tt // dil) == 0
        halo_per_tile = tt // halo
        halo_per_seq = seq // halo
        h0 = g * HEADS_PER_GROUP

        def main(cb, h0=h0):
            return pl.BlockSpec((1, tt, LANES), lambda b, h, i: (cb + h0 + h, b * n_t + i, 0))

        def prev(cb, h0=h0, halo=halo, hpt=halo_per_tile, hps=halo_per_seq):
            return pl.BlockSpec((1, halo, LANES),
                                lambda b, h, i: (cb + h0 + h, b * hps + jnp.maximum(i * hpt - 1, 0), 0))

        def nxt(cb, h0=h0, halo=halo, hpt=halo_per_tile, hps=halo_per_seq):
            return pl.BlockSpec((1, halo, LANES),
                                lambda b, h, i: (cb + h0 + h, b * hps + jnp.minimum((i + 1) * hpt, hps - 1), 0))

        in_specs += [main(CB_Q), prev(CB_K), main(CB_K), nxt(CB_K), prev(CB_V), main(CB_V), nxt(CB_V)]
    n_groups = len(ATTN_PATTERNS)
    kern = functools.partial(_attn_kernel, seq=seq, tt=tt, sub_max=cfg["attn_sub"])
    return pl.pallas_call(
        kern,
        grid=(batch, HEADS_PER_GROUP, n_t),
        in_specs=in_specs,
        out_specs=pl.BlockSpec((tt, LANES), lambda b, h, i: (b * n_t + i, h)),
        out_shape=jax.ShapeDtypeStruct((T, ATTN_OUT_W), BF16),
        scratch_shapes=[pltpu.VMEM((n_groups, tt, LANES), F32), pltpu.VMEM((n_groups, tt, LANES), F32)],
        compiler_params=_params("parallel", "parallel", "parallel"),
        name="attn",
    )(*([proj] * len(in_specs)))


def _merge_kernel(hf0_ref, hf1_ref, hb0_ref, hb1_ref, g_ref, gr_ref, ga_ref, attn_ref, wr_ref, wa_ref, out_ref):
    tm = out_ref.shape[0]
    hg = []
    for j in range(RNN_BLOCKS):
        grp, c = divmod(j, SUBLANES)
        hf_ref, hb_ref = ((hf0_ref, hb0_ref), (hf1_ref, hb1_ref))[grp]
        sel = pl.ds(c, tm, stride=SUBLANES)
        h = hf_ref[0, 0, sel, :] + hb_ref[0, 0, sel, :]
        hg.append((h * jax.nn.gelu(g_ref[j])).astype(BF16))
    y_rnn = jnp.dot(jnp.concatenate(hg, axis=-1), wr_ref[...], preferred_element_type=F32)
    y_attn = jnp.dot(attn_ref[...], wa_ref[...], preferred_element_type=F32)
    for j in range(RNN_BLOCKS):
        sl = slice(j * LANES, (j + 1) * LANES)
        mixed = _sigmoid(gr_ref[j]) * y_rnn[:, sl] + _sigmoid(ga_ref[j]) * y_attn[:, sl]
        out_ref[:, sl] = mixed.astype(BF16)


def _merge(proj, hrnn, attn, w_rnn_proj, w_attn_proj, cfg):
    T = proj.shape[1]
    tm = cfg["merge_tm"]
    blk16 = lambda cb: pl.BlockSpec((RNN_BLOCKS, tm, LANES), lambda m: (cb // RNN_BLOCKS, m, 0))
    hblk = lambda d, grp: pl.BlockSpec((1, 1, tm * SUBLANES, LANES), lambda m: (d, grp, m, 0))
    tok = lambda w: pl.BlockSpec((tm, w), lambda m: (m, 0))
    const = lambda a: pl.BlockSpec(a.shape, lambda m: (0, 0), pipeline_mode=pl.Buffered(1))
    return pl.pallas_call(
        _merge_kernel,
        grid=(T // tm,),
        in_specs=[hblk(0, 0), hblk(0, 1), hblk(1, 0), hblk(1, 1),
                  blk16(CB_GRNN), blk16(CB_GATE_RNN), blk16(CB_GATE_ATTN),
                  tok(ATTN_OUT_W), const(w_rnn_proj), const(w_attn_proj)],
        out_specs=tok(D_MODEL),
        out_shape=jax.ShapeDtypeStruct((T, D_MODEL), BF16),
        compiler_params=_params("parallel"),
        name="merge",
    )(hrnn, hrnn, hrnn, hrnn, proj, proj, proj, attn, w_rnn_proj, w_attn_proj)


def _mlp_kernel(x_ref, mix_ref, wo_ref, g_ref, fg_ref, wu_ref, wd_ref, o_ref, xn_ref, acc_ref):
    f = pl.program_id(1)

    @pl.when(f == 0)
    def _():
        h1 = x_ref[...] + jnp.dot(mix_ref[...], wo_ref[...], preferred_element_type=F32)
        xn_ref[...] = _rms(h1, g_ref[...]).astype(BF16)
        acc_ref[...] = h1

    u = jnp.dot(xn_ref[...], wu_ref[...], preferred_element_type=F32)
    act = jnp.square(jnp.maximum(u, 0.0)).astype(BF16)
    acc_ref[...] += jnp.dot(act, wd_ref[...], preferred_element_type=F32)

    @pl.when(f == pl.num_programs(1) - 1)
    def _():
        o_ref[...] = _rms(acc_ref[...], fg_ref[...])


def _mlp(x2, mixed, w_out, mlp_g, final_g, w_up, w_down, cfg):
    T = x2.shape[0]
    tm, tf = cfg["mlp_tm"], cfg["mlp_tf"]
    tok = pl.BlockSpec((tm, D_MODEL), lambda m, f: (m, 0))
    vec = pl.BlockSpec((1, D_MODEL), lambda m, f: (0, 0))
    return pl.pallas_call(
        _mlp_kernel,
        grid=(T // tm, D_FF // tf),
        in_specs=[
            tok, tok,
            pl.BlockSpec((D_MODEL, D_MODEL), lambda m, f: (0, 0), pipeline_mode=pl.Buffered(1)),
            vec, vec,
            pl.BlockSpec((D_MODEL, tf), lambda m, f: (0, f)),
            pl.BlockSpec((tf, D_MODEL), lambda m, f: (f, 0)),
        ],
        out_specs=tok,
        out_shape=jax.ShapeDtypeStruct((T, D_MODEL), F32),
        scratch_shapes=[pltpu.VMEM((tm, D_MODEL), BF16), pltpu.VMEM((tm, D_MODEL), F32)],
        compiler_params=_params("parallel", "arbitrary"),
        name="mlp",
    )(x2, mixed, w_out, mlp_g, final_g, w_up, w_down)


def _rope_lane_perm():
    dims = np.arange(HEAD_DIM)
    rest = dims[ROPE_DIM:]
    n_mid = LANES // 2 - ROPE_HALF
    return np.concatenate([dims[:ROPE_HALF], rest[:n_mid], dims[ROPE_HALF:ROPE_DIM], rest[n_mid:]])


def _rope_tables(seq):
    inv_freq = ROPE_THETA ** (-np.arange(0, ROPE_DIM, 2, dtype=np.float64) / ROPE_DIM)
    ang = np.arange(seq, dtype=np.float64)[:, None] * inv_freq[None, :]
    cos, sin = np.cos(ang), np.sin(ang)
    pad = LANES // 2 - ROPE_HALF
    ones, zeros = np.ones((seq, pad)), np.zeros((seq, pad))
    cos_t = np.concatenate([cos, ones, cos, ones], axis=-1)
    sin_t = np.concatenate([-sin, zeros, sin, zeros], axis=-1)
    return jnp.asarray(cos_t, F32), jnp.asarray(sin_t, F32)


def _layer(x, wts):
    batch, seq, _ = x.shape
    cfg = _tiles(seq)
    x2 = x.reshape(batch * seq, D_MODEL)
    cos_t, sin_t = _rope_tables(seq)
    proj = _inproj(x2, wts["mix_g"], wts["w_in"], cos_t, sin_t, seq, cfg)
    hrnn = _rglru(proj, wts["conv_w"], wts["conv_b"], wts["wg"], wts["bg"], wts["lam"], batch, seq, cfg)
    attn = _attention(proj, batch, seq, cfg)
    mixed = _merge(proj, hrnn, attn, wts["w_rnn_proj"], wts["w_attn_proj"], cfg)
    y = _mlp(x2, mixed, wts["w_out"], wts["mlp_g"], wts["final_g"], wts["w_up"], wts["w_down"], cfg)
    return y.reshape(batch, seq, D_MODEL)


def kernel(x_prompt, x_sample, mix_norm_g, w_in, conv_w, conv_b, lru_w_a, lru_b_a, lru_w_i, lru_b_i,
           lru_lambda, w_rnn_proj, w_attn_proj, w_out, mlp_norm_g, w_up, w_down, final_norm_g):
    assert mix_norm_g.shape[0] == 1, "single-layer stack"
    for window, dil in ATTN_PATTERNS:
        assert window // (2 * dil) == ATTN_RADIUS
    l = 0
    c0 = 2 * D_RNN
    c1 = c0 + 3 * ATTN_W
    w = w_in[l]
    perm = _rope_lane_perm()
    qk = w[:, c0:c0 + 2 * ATTN_W].reshape(D_MODEL, 2 * N_HEADS, HEAD_DIM)[:, :, perm].reshape(D_MODEL, 2 * ATTN_W)
    w_in_p = jnp.concatenate([w[:, :c0], w[:, c1:], qk, w[:, c0 + 2 * ATTN_W:c1]], axis=1).astype(BF16)
    nb = RNN_BLOCKS
    bw = D_RNN // nb
    wts = dict(
        mix_g=mix_norm_g[l][None, :],
        w_in=w_in_p,
        conv_w=conv_w[l].reshape(CONV_W, nb, bw).transpose(1, 0, 2),
        conv_b=conv_b[l].reshape(nb, 1, bw),
        wg=(0.5 * jnp.concatenate([lru_w_a[l], lru_w_i[l]], axis=-1)).astype(BF16),
        bg=0.5 * jnp.concatenate([lru_b_a[l].reshape(2, nb, 1, bw), lru_b_i[l].reshape(2, nb, 1, bw)], axis=-1),
        lam=lru_lambda[l].reshape(2, nb, 1, bw),
        w_rnn_proj=w_rnn_proj[l].astype(BF16),
        w_attn_proj=w_attn_proj[l].astype(BF16),
        w_out=w_out[l].astype(BF16),
        mlp_g=mlp_norm_g[l][None, :],
        final_g=final_norm_g[None, :],
        w_up=w_up[l].astype(BF16),
        w_down=w_down[l].astype(BF16),
    )
    return (_layer(x_prompt, wts), _layer(x_sample, wts))
```

```python
import functools

import numpy as np
import jax
import jax.numpy as jnp
from jax import lax
from jax.experimental import pallas as pl
from jax.experimental.pallas import tpu as pltpu

F32 = jnp.float32
BF16 = jnp.bfloat16

LANES = 128
SUBLANES = 8
MXU_COLS = 256
VMEM_LIMIT = 56 * 1024 * 1024

D_MODEL = 2048
D_RNN = D_MODEL
RNN_BLOCKS = 16
RNN_GROUPS = RNN_BLOCKS // SUBLANES
CONV_W = 4
LRU_C = 8.0
HEAD_DIM = 128
ATTN_PATTERNS = ((128, 1), (512, 4), (2048, 16))
HEADS_PER_GROUP = 4
N_HEADS = len(ATTN_PATTERNS) * HEADS_PER_GROUP
ATTN_W = N_HEADS * HEAD_DIM
ATTN_OUT_W = HEADS_PER_GROUP * HEAD_DIM
ATTN_RADIUS = 64
ROPE_DIM = HEAD_DIM // 4
ROPE_HALF = ROPE_DIM // 2
ROPE_THETA = 500000.0
D_FF = 4 * D_MODEL
EPS = 1e-6
NEG_INF = -1e30

CB_XRNN = 0
CB_GRNN = 16
CB_GATE_RNN = 32
CB_GATE_ATTN = 48
CB_Q = 64
CB_K = CB_Q + N_HEADS
CB_V = CB_K + N_HEADS
IN_BLOCKS = CB_V + N_HEADS
IN_COLS = IN_BLOCKS * LANES


def _tiles(seq):
    return dict(
        proj_tm=min(1024, seq), proj_tn=1280,
        rnn_chunk=min(512, seq),
        attn_tokens=min(2048, seq), attn_sub=128,
        merge_tm=min(256, seq),
        mlp_tm=min(512, seq), mlp_tf=1024,
    )


def _params(*sem):
    return pltpu.CompilerParams(dimension_semantics=sem, vmem_limit_bytes=VMEM_LIMIT)


def _rms(xf, g):
    return xf * lax.rsqrt(jnp.mean(xf * xf, axis=-1, keepdims=True) + EPS) * g


def _sigmoid(z):
    return 0.5 * jnp.tanh(0.5 * z) + 0.5


def _inproj_kernel(x_ref, g_ref, w_ref, cos_ref, sin_ref, o_ref, xn_ref):
    n = pl.program_id(1)

    @pl.when(n == 0)
    def _():
        xn_ref[...] = _rms(x_ref[...], g_ref[...]).astype(BF16)

    nblk = o_ref.shape[0]
    tiles = {}
    for tile in range(IN_BLOCKS // nblk):
        flags = tuple(CB_Q <= tile * nblk + j < CB_V for j in range(nblk))
        tiles.setdefault(flags, []).append(tile)

    for flags, members in tiles.items():
        cond = functools.reduce(jnp.logical_or, [n == t for t in members])

        @pl.when(cond)
        def _(flags=flags):
            for j2 in range(nblk // 2):
                cols = slice(2 * j2 * LANES, 2 * (j2 + 1) * LANES)
                r = jnp.dot(xn_ref[...], w_ref[:, cols], preferred_element_type=F32)
                for half in range(2):
                    j = 2 * j2 + half
                    t = r[:, half * LANES:(half + 1) * LANES]
                    if flags[j]:
                        t = t * cos_ref[...] + pltpu.roll(t, LANES // 2, 1) * sin_ref[...]
                    o_ref[j] = t


def _inproj(x2, g, w_in_p, cos_t, sin_t, seq, cfg):
    T = x2.shape[0]
    tm, tn = cfg["proj_tm"], cfg["proj_tn"]
    nb = tn // LANES
    assert IN_BLOCKS % nb == 0 and nb % 2 == 0
    tiles_per_seq = seq // tm
    kern = _inproj_kernel
    return pl.pallas_call(
        kern,
        grid=(T // tm, IN_COLS // tn),
        in_specs=[
            pl.BlockSpec((tm, D_MODEL), lambda m, n: (m, 0)),
            pl.BlockSpec((1, D_MODEL), lambda m, n: (0, 0)),
            pl.BlockSpec((D_MODEL, tn), lambda m, n: (0, n)),
            pl.BlockSpec((tm, LANES), lambda m, n: (m % tiles_per_seq, 0)),
            pl.BlockSpec((tm, LANES), lambda m, n: (m % tiles_per_seq, 0)),
        ],
        out_specs=pl.BlockSpec((nb, tm, LANES), lambda m, n: (n, m, 0)),
        out_shape=jax.ShapeDtypeStruct((IN_BLOCKS, T, LANES), F32),
        scratch_shapes=[pltpu.VMEM((tm, D_MODEL), BF16)],
        compiler_params=_params("parallel", "arbitrary"),
        name="inproj",
    )(x2, g, w_in_p, cos_t, sin_t)


def _rglru_kernel(x_ref, xp_ref, xn_ref, cw_ref, cb_ref, wg_ref, bg_ref, lam_ref, h_ref,
                  a_scr, u_scr, state_ref, *, chunk, n_chunks):
    d = pl.program_id(0)
    ci = pl.program_id(2)
    c_seq = jnp.where(d == 0, ci, n_chunks - 1 - ci)
    ext_rows = chunk + 2 * SUBLANES
    mid = slice(SUBLANES, SUBLANES + chunk)

    @pl.when(ci == 0)
    def _():
        state_ref[...] = jnp.zeros_like(state_ref)

    def gate_block(c, carry):
        x = x_ref[c]
        prev = jnp.where(c_seq == 0, 0.0, xp_ref[c])
        nxt = jnp.where(c_seq == n_chunks - 1, 0.0, xn_ref[c])
        cw, cb, wg, bg, lam = cw_ref[c], cb_ref[c], wg_ref[0, c], bg_ref[0, c], lam_ref[0, c]
        ext = jnp.concatenate([prev, x, nxt], axis=0)
        xc = (cb + pltpu.roll(ext, 2, 0)[mid] * cw[0:1] + pltpu.roll(ext, 1, 0)[mid] * cw[1:2]
              + x * cw[2:3] + pltpu.roll(ext, ext_rows - 1, 0)[mid] * cw[3:4])
        tz = jnp.tanh(jnp.dot(xc.astype(BF16), wg, preferred_element_type=F32) + bg)
        i = 0.5 * tz[:, LANES:] + 0.5
        z = -lam
        softplus = jnp.maximum(z, 0.0) + jnp.log1p(jnp.exp(-jnp.abs(z)))
        half_rate = (-0.5 * LRU_C) * softplus
        log_a = half_rate * tz[:, :LANES] + half_rate
        a = jnp.exp(log_a)
        u = jnp.sqrt(-jnp.tanh(log_a) * (a * a + 1.0)) * (i * xc)
        grp, sub = c // SUBLANES, c % SUBLANES
        a_scr[grp, pl.ds(sub, chunk, stride=SUBLANES), :] = a
        u_scr[grp, pl.ds(sub, chunk, stride=SUBLANES), :] = u
        return carry

    lax.fori_loop(0, RNN_BLOCKS, gate_block, 0)

    def step(t, hs):
        r0 = pl.multiple_of(t * SUBLANES, SUBLANES)
        out = []
        for grp, h in enumerate(hs):
            h = a_scr[grp, pl.ds(r0, SUBLANES), :] * h + u_scr[grp, pl.ds(r0, SUBLANES), :]
            h_ref[0, grp, pl.ds(r0, SUBLANES), :] = h
            out.append(h)
        return tuple(out)

    def scan(order):
        hs = lax.fori_loop(0, chunk, lambda k, hs: step(order(k), hs),
                           tuple(state_ref[grp] for grp in range(RNN_GROUPS)), unroll=8)
        for grp, h in enumerate(hs):
            state_ref[grp] = h

    @pl.when(d == 0)
    def _():
        scan(lambda k: k)

    @pl.when(d == 1)
    def _():
        scan(lambda k: chunk - 1 - k)


def _rglru(proj, conv_w, conv_b, wg, bg, lam, batch, seq, cfg):
    T = proj.shape[1]
    chunk = cfg["rnn_chunk"]
    n_chunks = seq // chunk
    halo_per_chunk = chunk // SUBLANES
    halo_per_seq = seq // SUBLANES

    def pos(d, ci):
        return ci + d * (n_chunks - 1 - 2 * ci)

    per_block = lambda shape: pl.BlockSpec((RNN_BLOCKS,) + shape, lambda d, b, ci: (0,) * (1 + len(shape)))
    per_dir_block = lambda shape: pl.BlockSpec((1, RNN_BLOCKS) + shape,
                                               lambda d, b, ci: (d,) + (0,) * (1 + len(shape)))
    kern = functools.partial(_rglru_kernel, chunk=chunk, n_chunks=n_chunks)
    return pl.pallas_call(
        kern,
        grid=(2, batch, n_chunks),
        in_specs=[
            pl.BlockSpec((RNN_BLOCKS, chunk, LANES), lambda d, b, ci: (0, b * n_chunks + pos(d, ci), 0)),
            pl.BlockSpec((RNN_BLOCKS, SUBLANES, LANES),
                         lambda d, b, ci: (0, b * halo_per_seq + jnp.maximum(pos(d, ci) * halo_per_chunk - 1, 0), 0)),
            pl.BlockSpec((RNN_BLOCKS, SUBLANES, LANES),
                         lambda d, b, ci: (0, b * halo_per_seq
                                           + jnp.minimum((pos(d, ci) + 1) * halo_per_chunk, halo_per_seq - 1), 0)),
            per_block((CONV_W, LANES)),
            per_block((1, LANES)),
            per_dir_block((LANES, 2 * LANES)),
            per_dir_block((1, 2 * LANES)),
            per_dir_block((1, LANES)),
        ],
        out_specs=pl.BlockSpec((1, RNN_GROUPS, chunk * SUBLANES, LANES),
                               lambda d, b, ci: (d, 0, b * n_chunks + pos(d, ci), 0)),
        out_shape=jax.ShapeDtypeStruct((2, RNN_GROUPS, T * SUBLANES, LANES), F32),
        scratch_shapes=[pltpu.VMEM((RNN_GROUPS, chunk * SUBLANES, LANES), F32)] * 2
                       + [pltpu.VMEM((RNN_GROUPS, SUBLANES, LANES), F32)],
        compiler_params=_params("parallel", "parallel", "arbitrary"),
        name="rglru",
    )(proj, proj, proj, conv_w, conv_b, wg, bg, lam)


ATTN_REFS_PER_GROUP = 7
ATTN_COMBINE_ROWS = 256


def _attn_kernel(*refs, seq, tt, sub_max):
    n_groups = len(ATTN_PATTERNS)
    in_refs = refs[:ATTN_REFS_PER_GROUP * n_groups]
    out_ref, o_scr, l_scr = refs[ATTN_REFS_PER_GROUP * n_groups:]
    i = pl.program_id(2)
    scale = 1.0 / float(np.sqrt(HEAD_DIM))

    for g, (_, dil) in enumerate(ATTN_PATTERNS):
        q_ref, kp_ref, km_ref, kn_ref, vp_ref, vm_ref, vn_ref = in_refs[ATTN_REFS_PER_GROUP * g:ATTN_REFS_PER_GROUP * (g + 1)]
        tq = tt // dil
        sub = min(sub_max, tq)
        sub_len = seq // dil
        nk = sub + 2 * ATTN_RADIUS
        qi = lax.broadcasted_iota(jnp.int32, (sub, nk), 0)
        kj = lax.broadcasted_iota(jnp.int32, (sub, nk), 1)
        band = jnp.abs(kj - ATTN_RADIUS - qi) <= ATTN_RADIUS

        def rows(ref, r, n, dil=dil):
            return ref[0, pl.ds(r, n, stride=dil), :] if dil > 1 else ref[0, pl.ds(r, n), :]

        def ext(p_ref, m_ref, n_ref, r, tq=tq, rows=rows):
            parts = [rows(p_ref, r, ATTN_RADIUS), rows(m_ref, r, tq), rows(n_ref, r, ATTN_RADIUS)]
            return jnp.concatenate(parts, axis=0).astype(BF16)

        for r in range(dil):
            q = rows(q_ref, r, tq).astype(BF16)
            kext = ext(kp_ref, km_ref, kn_ref, r)
            vext = ext(vp_ref, vm_ref, vn_ref, r)
            for sb in range(tq // sub):
                r0 = sb * sub
                s = lax.dot_general(q[r0:r0 + sub], kext[r0:r0 + nk], (((1,), (1,)), ((), ())),
                                    preferred_element_type=F32) * scale
                kpos = i * tq + (r0 - ATTN_RADIUS) + kj
                mask = band & (kpos >= 0) & (kpos < sub_len)
                s = jnp.where(mask, s, NEG_INF)
                m = jnp.max(s, axis=-1, keepdims=True)
                p = jnp.exp(s - m)
                den = jnp.sum(p, axis=-1, keepdims=True)
                o = jnp.dot(p.astype(BF16), vext[r0:r0 + nk], preferred_element_type=F32) / den
                lse = jnp.broadcast_to(m + jnp.log(den), (sub, LANES))
                dst = pl.ds(r + r0 * dil, sub, stride=dil) if dil > 1 else pl.ds(r0, sub)
                o_scr[g, dst, :] = o
                l_scr[g, dst, :] = lse

    rows_c = min(ATTN_COMBINE_ROWS, tt)

    def combine(k, carry):
        sl = pl.ds(pl.multiple_of(k * rows_c, rows_c), rows_c)
        ls = [l_scr[g, sl, :] for g in range(n_groups)]
        mx = functools.reduce(jnp.maximum, ls)
        es = [jnp.exp(l - mx) for l in ls]
        num = sum(e * o_scr[g, sl, :] for g, e in enumerate(es))
        out_ref[sl, :] = (num / sum(es)).astype(BF16)
        return carry

    lax.fori_loop(0, tt // rows_c, combine, 0)


def _attention(proj, batch, seq, cfg):
    T = proj.shape[1]
    tt = cfg["attn_tokens"]
    n_t = seq // tt
    assert seq % tt == 0
    in_specs = []
    for g, (_, dil) in enumerate(ATTN_PATTERNS):
        halo = ATTN_RADIUS * dil
        assert tt % halo == 0 and (tt // dil) % min(cfg["attn_sub"], tt // dil) == 0
        halo_per_tile = tt // halo
        halo_per_seq = seq // halo
        h0 = g * HEADS_PER_GROUP

        def main(cb, h0=h0):
            return pl.BlockSpec((1, tt, LANES), lambda b, h, i: (cb + h0 + h, b * n_t + i, 0))

        def prev(cb, h0=h0, halo=halo, hpt=halo_per_tile, hps=halo_per_seq):
            return pl.BlockSpec((1, halo, LANES),
                                lambda b, h, i: (cb + h0 + h, b * hps + jnp.maximum(i * hpt - 1, 0), 0))

        def nxt(cb, h0=h0, halo=halo, hpt=halo_per_tile, hps=halo_per_seq):
            return pl.BlockSpec((1, halo, LANES),
                                lambda b, h, i: (cb + h0 + h, b * hps + jnp.minimum((i + 1) * hpt, hps - 1), 0))

        in_specs += [main(CB_Q), prev(CB_K), main(CB_K), nxt(CB_K), prev(CB_V), main(CB_V), nxt(CB_V)]
    n_groups = len(ATTN_PATTERNS)
    kern = functools.partial(_attn_kernel, seq=seq, tt=tt, sub_max=cfg["attn_sub"])
    return pl.pallas_call(
        kern,
        grid=(batch, HEADS_PER_GROUP, n_t),
        in_specs=in_specs,
        out_specs=pl.BlockSpec((tt, LANES), lambda b, h, i: (b * n_t + i, h)),
        out_shape=jax.ShapeDtypeStruct((T, ATTN_OUT_W), BF16),
        scratch_shapes=[pltpu.VMEM((n_groups, tt, LANES), F32), pltpu.VMEM((n_groups, tt, LANES), F32)],
        compiler_params=_params("parallel", "parallel", "parallel"),
        name="attn",
    )(*([proj] * len(in_specs)))


def _merge_kernel(hf0_ref, hf1_ref, hb0_ref, hb1_ref, g_ref, gr_ref, ga_ref, attn_ref, wr_ref, wa_ref, out_ref):
    tm = out_ref.shape[0]
    hg = []
    for j in range(RNN_BLOCKS):
        grp, c = divmod(j, SUBLANES)
        hf_ref, hb_ref = ((hf0_ref, hb0_ref), (hf1_ref, hb1_ref))[grp]
        sel = pl.ds(c, tm, stride=SUBLANES)
        h = hf_ref[0, 0, sel, :] + hb_ref[0, 0, sel, :]
        hg.append((h * jax.nn.gelu(g_ref[j])).astype(BF16))
    y_rnn = jnp.dot(jnp.concatenate(hg, axis=-1), wr_ref[...], preferred_element_type=F32)
    y_attn = jnp.dot(attn_ref[...], wa_ref[...], preferred_element_type=F32)
    for j in range(RNN_BLOCKS):
        sl = slice(j * LANES, (j + 1) * LANES)
        mixed = _sigmoid(gr_ref[j]) * y_rnn[:, sl] + _sigmoid(ga_ref[j]) * y_attn[:, sl]
        out_ref[:, sl] = mixed.astype(BF16)


def _merge(proj, hrnn, attn, w_rnn_proj, w_attn_proj, cfg):
    T = proj.shape[1]
    tm = cfg["merge_tm"]
    blk16 = lambda cb: pl.BlockSpec((RNN_BLOCKS, tm, LANES), lambda m: (cb // RNN_BLOCKS, m, 0))
    hblk = lambda d, grp: pl.BlockSpec((1, 1, tm * SUBLANES, LANES), lambda m: (d, grp, m, 0))
    tok = lambda w: pl.BlockSpec((tm, w), lambda m: (m, 0))
    const = lambda a: pl.BlockSpec(a.shape, lambda m: (0, 0), pipeline_mode=pl.Buffered(1))
    return pl.pallas_call(
        _merge_kernel,
        grid=(T // tm,),
        in_specs=[hblk(0, 0), hblk(0, 1), hblk(1, 0), hblk(1, 1),
                  blk16(CB_GRNN), blk16(CB_GATE_RNN), blk16(CB_GATE_ATTN),
                  tok(ATTN_OUT_W), const(w_rnn_proj), const(w_attn_proj)],
        out_specs=tok(D_MODEL),
        out_shape=jax.ShapeDtypeStruct((T, D_MODEL), BF16),
        compiler_params=_params("parallel"),
        name="merge",
    )(hrnn, hrnn, hrnn, hrnn, proj, proj, proj, attn, w_rnn_proj, w_attn_proj)


def _mlp_kernel(x_ref, mix_ref, wo_ref, g_ref, fg_ref, wu_ref, wd_ref, o_ref, xn_ref, acc_ref, act_ref):
    f = pl.program_id(1)
    tm = acc_ref.shape[0]
    col_groups = lambda width: [slice(j * MXU_COLS, (j + 1) * MXU_COLS) for j in range(width // MXU_COLS)]

    @pl.when(f == 0)
    def _():
        sq = jnp.zeros((tm, LANES), F32)
        for cols in col_groups(D_MODEL):
            h1 = x_ref[:, cols] + jnp.dot(mix_ref[...], wo_ref[:, cols], preferred_element_type=F32)
            acc_ref[:, cols] = h1
            for k in range(MXU_COLS // LANES):
                part = h1[:, k * LANES:(k + 1) * LANES]
                sq = sq + part * part
        inv = lax.rsqrt(jnp.sum(sq, axis=-1, keepdims=True) * (1.0 / D_MODEL) + EPS)
        for cols in col_groups(D_MODEL):
            xn_ref[:, cols] = (acc_ref[:, cols] * inv * g_ref[:, cols]).astype(BF16)

    for cols in col_groups(act_ref.shape[1]):
        u = jnp.dot(xn_ref[...], wu_ref[:, cols], preferred_element_type=F32)
        act_ref[:, cols] = jnp.square(jnp.maximum(u, 0.0)).astype(BF16)
    for cols in col_groups(D_MODEL):
        acc_ref[:, cols] += jnp.dot(act_ref[...], wd_ref[:, cols], preferred_element_type=F32)

    @pl.when(f == pl.num_programs(1) - 1)
    def _():
        o_ref[...] = _rms(acc_ref[...], fg_ref[...])


def _mlp(x2, mixed, w_out, mlp_g, final_g, w_up, w_down, cfg):
    T = x2.shape[0]
    tm, tf = cfg["mlp_tm"], cfg["mlp_tf"]
    tok = pl.BlockSpec((tm, D_MODEL), lambda m, f: (m, 0))
    vec = pl.BlockSpec((1, D_MODEL), lambda m, f: (0, 0))
    return pl.pallas_call(
        _mlp_kernel,
        grid=(T // tm, D_FF // tf),
        in_specs=[
            tok, tok,
            pl.BlockSpec((D_MODEL, D_MODEL), lambda m, f: (0, 0), pipeline_mode=pl.Buffered(1)),
            vec, vec,
            pl.BlockSpec((D_MODEL, tf), lambda m, f: (0, f)),
            pl.BlockSpec((tf, D_MODEL), lambda m, f: (f, 0)),
        ],
        out_specs=tok,
        out_shape=jax.ShapeDtypeStruct((T, D_MODEL), F32),
        scratch_shapes=[pltpu.VMEM((tm, D_MODEL), BF16), pltpu.VMEM((tm, D_MODEL), F32),
                        pltpu.VMEM((tm, tf), BF16)],
        compiler_params=_params("parallel", "arbitrary"),
        name="mlp",
    )(x2, mixed, w_out, mlp_g, final_g, w_up, w_down)


def _rope_lane_perm():
    dims = np.arange(HEAD_DIM)
    rest = dims[ROPE_DIM:]
    n_mid = LANES // 2 - ROPE_HALF
    return np.concatenate([dims[:ROPE_HALF], rest[:n_mid], dims[ROPE_HALF:ROPE_DIM], rest[n_mid:]])


def _rope_tables(seq):
    inv_freq = ROPE_THETA ** (-np.arange(0, ROPE_DIM, 2, dtype=np.float64) / ROPE_DIM)
    ang = np.arange(seq, dtype=np.float64)[:, None] * inv_freq[None, :]
    cos, sin = np.cos(ang), np.sin(ang)
    pad = LANES // 2 - ROPE_HALF
    ones, zeros = np.ones((seq, pad)), np.zeros((seq, pad))
    cos_t = np.concatenate([cos, ones, cos, ones], axis=-1)
    sin_t = np.concatenate([-sin, zeros, sin, zeros], axis=-1)
    return jnp.asarray(cos_t, F32), jnp.asarray(sin_t, F32)


def _layer(x, wts):
    batch, seq, _ = x.shape
    cfg = _tiles(seq)
    x2 = x.reshape(batch * seq, D_MODEL)
    cos_t, sin_t = _rope_tables(seq)
    proj = _inproj(x2, wts["mix_g"], wts["w_in"], cos_t, sin_t, seq, cfg)
    hrnn = _rglru(proj, wts["conv_w"], wts["conv_b"], wts["wg"], wts["bg"], wts["lam"], batch, seq, cfg)
    attn = _attention(proj, batch, seq, cfg)
    mixed = _merge(proj, hrnn, attn, wts["w_rnn_proj"], wts["w_attn_proj"], cfg)
    y = _mlp(x2, mixed, wts["w_out"], wts["mlp_g"], wts["final_g"], wts["w_up"], wts["w_down"], cfg)
    return y.reshape(batch, seq, D_MODEL)


def kernel(x_prompt, x_sample, mix_norm_g, w_in, conv_w, conv_b, lru_w_a, lru_b_a, lru_w_i, lru_b_i,
           lru_lambda, w_rnn_proj, w_attn_proj, w_out, mlp_norm_g, w_up, w_down, final_norm_g):
    assert mix_norm_g.shape[0] == 1, "single-layer stack"
    for window, dil in ATTN_PATTERNS:
        assert window // (2 * dil) == ATTN_RADIUS
    l = 0
    c0 = 2 * D_RNN
    c1 = c0 + 3 * ATTN_W
    w = w_in[l]
    perm = _rope_lane_perm()
    qk = w[:, c0:c0 + 2 * ATTN_W].reshape(D_MODEL, 2 * N_HEADS, HEAD_DIM)[:, :, perm].reshape(D_MODEL, 2 * ATTN_W)
    w_in_p = jnp.concatenate([w[:, :c0], w[:, c1:], qk, w[:, c0 + 2 * ATTN_W:c1]], axis=1).astype(BF16)
    nb = RNN_BLOCKS
    bw = D_RNN // nb
    wts = dict(
        mix_g=mix_norm_g[l][None, :],
        w_in=w_in_p,
        conv_w=conv_w[l].reshape(CONV_W, nb, bw).transpose(1, 0, 2),
        conv_b=conv_b[l].reshape(nb, 1, bw),
        wg=(0.5 * jnp.concatenate([lru_w_a[l], lru_w_i[l]], axis=-1)).astype(BF16),
        bg=0.5 * jnp.concatenate([lru_b_a[l].reshape(2, nb, 1, bw), lru_b_i[l].reshape(2, nb, 1, bw)], axis=-1),
        lam=lru_lambda[l].reshape(2, nb, 1, bw),
        w_rnn_proj=w_rnn_proj[l].astype(BF16),
        w_attn_proj=w_attn_proj[l].astype(BF16),
        w_out=w_out[l].astype(BF16),
        mlp_g=mlp_norm_g[l][None, :],
        final_g=final_norm_g[None, :],
        w_up=w_up[l].astype(BF16),
        w_down=w_down[l].astype(BF16),
    )
    return (_layer(x_prompt, wts), _layer(x_sample, wts))
```

```python
import functools

import numpy as np
import jax
import jax.numpy as jnp
from jax import lax
from jax.experimental import pallas as pl
from jax.experimental.pallas import tpu as pltpu

F32 = jnp.float32
BF16 = jnp.bfloat16

LANES = 128
SUBLANES = 8
MXU_COLS = 256
VMEM_LIMIT = 56 * 1024 * 1024

D_MODEL = 2048
D_RNN = D_MODEL
RNN_BLOCKS = 16
RNN_GROUPS = RNN_BLOCKS // SUBLANES
SCAN_UNROLL = 8
CONV_W = 4
LRU_C = 8.0
HEAD_DIM = 128
ATTN_PATTERNS = ((128, 1), (512, 4), (2048, 16))
HEADS_PER_GROUP = 4
N_HEADS = len(ATTN_PATTERNS) * HEADS_PER_GROUP
ATTN_W = N_HEADS * HEAD_DIM
ATTN_OUT_W = HEADS_PER_GROUP * HEAD_DIM
ATTN_RADIUS = 64
ROPE_DIM = HEAD_DIM // 4
ROPE_HALF = ROPE_DIM // 2
ROPE_THETA = 500000.0
D_FF = 4 * D_MODEL
EPS = 1e-6
NEG_INF = -1e30

CB_XRNN = 0
CB_GRNN = 16
CB_GATE_RNN = 32
CB_GATE_ATTN = 48
CB_Q = 64
CB_K = CB_Q + N_HEADS
CB_V = CB_K + N_HEADS
IN_BLOCKS = CB_V + N_HEADS
IN_COLS = IN_BLOCKS * LANES


def _tiles(seq):
    return dict(
        proj_tm=min(1024, seq), proj_tn=1280,
        rnn_chunk=min(512, seq),
        attn_tokens=min(2048, seq), attn_sub=128,
        merge_tm=min(256, seq),
        mlp_tm=min(512, seq), mlp_tf=1024,
    )


def _params(*sem):
    return pltpu.CompilerParams(dimension_semantics=sem, vmem_limit_bytes=VMEM_LIMIT)


def _rms(xf, g):
    return xf * lax.rsqrt(jnp.mean(xf * xf, axis=-1, keepdims=True) + EPS) * g


def _sigmoid(z):
    return 0.5 * jnp.tanh(0.5 * z) + 0.5


def _inproj_kernel(x_ref, g_ref, w_ref, cos_ref, sin_ref, o_ref, xn_ref):
    n = pl.program_id(1)

    @pl.when(n == 0)
    def _():
        xn_ref[...] = _rms(x_ref[...], g_ref[...]).astype(BF16)

    nblk = o_ref.shape[0]
    tiles = {}
    for tile in range(IN_BLOCKS // nblk):
        flags = tuple(CB_Q <= tile * nblk + j < CB_V for j in range(nblk))
        tiles.setdefault(flags, []).append(tile)

    for flags, members in tiles.items():
        cond = functools.reduce(jnp.logical_or, [n == t for t in members])

        @pl.when(cond)
        def _(flags=flags):
            for j2 in range(nblk // 2):
                cols = slice(2 * j2 * LANES, 2 * (j2 + 1) * LANES)
                r = jnp.dot(xn_ref[...], w_ref[:, cols], preferred_element_type=F32)
                for half in range(2):
                    j = 2 * j2 + half
                    t = r[:, half * LANES:(half + 1) * LANES]
                    if flags[j]:
                        t = t * cos_ref[...] + pltpu.roll(t, LANES // 2, 1) * sin_ref[...]
                    o_ref[j] = t


def _inproj(x2, g, w_in_p, cos_t, sin_t, seq, cfg):
    T = x2.shape[0]
    tm, tn = cfg["proj_tm"], cfg["proj_tn"]
    nb = tn // LANES
    assert IN_BLOCKS % nb == 0 and nb % 2 == 0
    tiles_per_seq = seq // tm
    kern = _inproj_kernel
    return pl.pallas_call(
        kern,
        grid=(T // tm, IN_COLS // tn),
        in_specs=[
            pl.BlockSpec((tm, D_MODEL), lambda m, n: (m, 0)),
            pl.BlockSpec((1, D_MODEL), lambda m, n: (0, 0)),
            pl.BlockSpec((D_MODEL, tn), lambda m, n: (0, n)),
            pl.BlockSpec((tm, LANES), lambda m, n: (m % tiles_per_seq, 0)),
            pl.BlockSpec((tm, LANES), lambda m, n: (m % tiles_per_seq, 0)),
        ],
        out_specs=pl.BlockSpec((nb, tm, LANES), lambda m, n: (n, m, 0)),
        out_shape=jax.ShapeDtypeStruct((IN_BLOCKS, T, LANES), F32),
        scratch_shapes=[pltpu.VMEM((tm, D_MODEL), BF16)],
        compiler_params=_params("parallel", "arbitrary"),
        name="inproj",
    )(x2, g, w_in_p, cos_t, sin_t)


def _rglru_kernel(*refs, chunk, n_chunks, reverse):
    x_ref, xp_ref, xn_ref, cw_ref, cb_ref, wg_ref, bg_ref, lam_ref = refs[:8]
    other_ref = refs[8] if reverse else None
    h_ref, a_scr, u_scr, state_ref = refs[-4:]
    ci = pl.program_id(1)
    c_seq = n_chunks - 1 - ci if reverse else ci
    ext_rows = chunk + 2 * SUBLANES
    mid = slice(SUBLANES, SUBLANES + chunk)

    @pl.when(ci == 0)
    def _():
        state_ref[...] = jnp.zeros_like(state_ref)

    def gate_block(c, carry):
        x = x_ref[c]
        prev = jnp.where(c_seq == 0, 0.0, xp_ref[c])
        nxt = jnp.where(c_seq == n_chunks - 1, 0.0, xn_ref[c])
        cw, cb, wg, bg, lam = cw_ref[c], cb_ref[c], wg_ref[0, c], bg_ref[0, c], lam_ref[0, c]
        ext = jnp.concatenate([prev, x, nxt], axis=0)
        xc = (cb + pltpu.roll(ext, 2, 0)[mid] * cw[0:1] + pltpu.roll(ext, 1, 0)[mid] * cw[1:2]
              + x * cw[2:3] + pltpu.roll(ext, ext_rows - 1, 0)[mid] * cw[3:4])
        tz = jnp.tanh(jnp.dot(xc.astype(BF16), wg, preferred_element_type=F32) + bg)
        i = 0.5 * tz[:, LANES:] + 0.5
        z = -lam
        softplus = jnp.maximum(z, 0.0) + jnp.log1p(jnp.exp(-jnp.abs(z)))
        half_rate = (-0.5 * LRU_C) * softplus
        log_a = half_rate * tz[:, :LANES] + half_rate
        a = jnp.exp(log_a)
        u = jnp.sqrt(-jnp.tanh(log_a) * (a * a + 1.0)) * (i * xc)
        grp, sub = c // SUBLANES, c % SUBLANES
        a_scr[grp, pl.ds(sub, chunk, stride=SUBLANES), :] = a
        u_scr[grp, pl.ds(sub, chunk, stride=SUBLANES), :] = u
        return carry

    lax.fori_loop(0, RNN_BLOCKS, gate_block, 0)

    def steps(kb, hs):
        hs = list(hs)
        ts = [kb * SCAN_UNROLL + k for k in range(SCAN_UNROLL)]
        rows = [pl.ds(pl.multiple_of((chunk - 1 - t if reverse else t) * SUBLANES, SUBLANES), SUBLANES) for t in ts]
        groups = range(RNN_GROUPS)
        a = [[a_scr[grp, r, :] for r in rows] for grp in groups]
        u = [[u_scr[grp, r, :] for r in rows] for grp in groups]
        other = [[other_ref[grp, r, :] for r in rows] for grp in groups] if reverse else None
        for k, r in enumerate(rows):
            for grp in groups:
                hs[grp] = a[grp][k] * hs[grp] + u[grp][k]
                h_ref[grp, r, :] = hs[grp] + other[grp][k] if reverse else hs[grp]
        return tuple(hs)

    hs = lax.fori_loop(0, chunk // SCAN_UNROLL, steps, tuple(state_ref[grp] for grp in range(RNN_GROUPS)))
    for grp, h in enumerate(hs):
        state_ref[grp] = h


def _rglru(proj, conv_w, conv_b, wg, bg, lam, batch, seq, cfg, reverse, other=None):
    T = proj.shape[1]
    chunk = cfg["rnn_chunk"]
    n_chunks = seq // chunk
    halo_per_chunk = chunk // SUBLANES
    halo_per_seq = seq // SUBLANES
    d = int(reverse)

    def pos(ci):
        return n_chunks - 1 - ci if reverse else ci

    per_block = lambda shape: pl.BlockSpec((RNN_BLOCKS,) + shape, lambda b, ci: (0,) * (1 + len(shape)))
    per_dir_block = lambda shape: pl.BlockSpec((1, RNN_BLOCKS) + shape, lambda b, ci: (d,) + (0,) * (1 + len(shape)))
    state_spec = pl.BlockSpec((RNN_GROUPS, chunk * SUBLANES, LANES), lambda b, ci: (0, b * n_chunks + pos(ci), 0))
    kern = functools.partial(_rglru_kernel, chunk=chunk, n_chunks=n_chunks, reverse=reverse)
    return pl.pallas_call(
        kern,
        grid=(batch, n_chunks),
        in_specs=[
            pl.BlockSpec((RNN_BLOCKS, chunk, LANES), lambda b, ci: (0, b * n_chunks + pos(ci), 0)),
            pl.BlockSpec((RNN_BLOCKS, SUBLANES, LANES),
                         lambda b, ci: (0, b * halo_per_seq + jnp.maximum(pos(ci) * halo_per_chunk - 1, 0), 0)),
            pl.BlockSpec((RNN_BLOCKS, SUBLANES, LANES),
                         lambda b, ci: (0, b * halo_per_seq
                                        + jnp.minimum((pos(ci) + 1) * halo_per_chunk, halo_per_seq - 1), 0)),
            per_block((CONV_W, LANES)),
            per_block((1, LANES)),
            per_dir_block((LANES, 2 * LANES)),
            per_dir_block((1, 2 * LANES)),
            per_dir_block((1, LANES)),
        ] + ([state_spec] if reverse else []),
        out_specs=state_spec,
        out_shape=jax.ShapeDtypeStruct((RNN_GROUPS, T * SUBLANES, LANES), F32),
        scratch_shapes=[pltpu.VMEM((RNN_GROUPS, chunk * SUBLANES, LANES), F32)] * 2
                       + [pltpu.VMEM((RNN_GROUPS, SUBLANES, LANES), F32)],
        compiler_params=_params("parallel", "arbitrary"),
        name="rglru_bwd" if reverse else "rglru_fwd",
    )(proj, proj, proj, conv_w, conv_b, wg, bg, lam, *([other] if reverse else []))


ATTN_REFS_PER_GROUP = 7
ATTN_COMBINE_ROWS = 256


def _attn_kernel(*refs, seq, tt, sub_max):
    n_groups = len(ATTN_PATTERNS)
    in_refs = refs[:ATTN_REFS_PER_GROUP * n_groups]
    out_ref, o_scr, l_scr = refs[ATTN_REFS_PER_GROUP * n_groups:]
    i = pl.program_id(2)
    scale = 1.0 / float(np.sqrt(HEAD_DIM))

    for g, (_, dil) in enumerate(ATTN_PATTERNS):
        q_ref, kp_ref, km_ref, kn_ref, vp_ref, vm_ref, vn_ref = in_refs[ATTN_REFS_PER_GROUP * g:ATTN_REFS_PER_GROUP * (g + 1)]
        tq = tt // dil
        sub = min(sub_max, tq)
        sub_len = seq // dil
        nk = sub + 2 * ATTN_RADIUS
        qi = lax.broadcasted_iota(jnp.int32, (sub, nk), 0)
        kj = lax.broadcasted_iota(jnp.int32, (sub, nk), 1)
        band = jnp.abs(kj - ATTN_RADIUS - qi) <= ATTN_RADIUS

        def rows(ref, r, n, dil=dil):
            return ref[0, pl.ds(r, n, stride=dil), :] if dil > 1 else ref[0, pl.ds(r, n), :]

        def ext(p_ref, m_ref, n_ref, r, tq=tq, rows=rows):
            parts = [rows(p_ref, r, ATTN_RADIUS), rows(m_ref, r, tq), rows(n_ref, r, ATTN_RADIUS)]
            return jnp.concatenate(parts, axis=0).astype(BF16)

        for r in range(dil):
            q = rows(q_ref, r, tq).astype(BF16)
            kext = ext(kp_ref, km_ref, kn_ref, r)
            vext = ext(vp_ref, vm_ref, vn_ref, r)
            for sb in range(tq // sub):
                r0 = sb * sub
                s = lax.dot_general(q[r0:r0 + sub], kext[r0:r0 + nk], (((1,), (1,)), ((), ())),
                                    preferred_element_type=F32) * scale
                kpos = i * tq + (r0 - ATTN_RADIUS) + kj
                mask = band & (kpos >= 0) & (kpos < sub_len)
                s = jnp.where(mask, s, NEG_INF)
                m = jnp.max(s, axis=-1, keepdims=True)
                p = jnp.exp(s - m)
                den = jnp.sum(p, axis=-1, keepdims=True)
                o = jnp.dot(p.astype(BF16), vext[r0:r0 + nk], preferred_element_type=F32) / den
                lse = jnp.broadcast_to(m + jnp.log(den), (sub, LANES))
                dst = pl.ds(r + r0 * dil, sub, stride=dil) if dil > 1 else pl.ds(r0, sub)
                o_scr[g, dst, :] = o
                l_scr[g, dst, :] = lse

    rows_c = min(ATTN_COMBINE_ROWS, tt)

    def combine(k, carry):
        sl = pl.ds(pl.multiple_of(k * rows_c, rows_c), rows_c)
        ls = [l_scr[g, sl, :] for g in range(n_groups)]
        mx = functools.reduce(jnp.maximum, ls)
        es = [jnp.exp(l - mx) for l in ls]
        num = sum(e * o_scr[g, sl, :] for g, e in enumerate(es))
        out_ref[sl, :] = (num / sum(es)).astype(BF16)
        return carry

    lax.fori_loop(0, tt // rows_c, combine, 0)


def _attention(proj, batch, seq, cfg):
    T = proj.shape[1]
    tt = cfg["attn_tokens"]
    n_t = seq // tt
    assert seq % tt == 0
    in_specs = []
    for g, (_, dil) in enumerate(ATTN_PATTERNS):
        halo = ATTN_RADIUS * dil
        assert tt % halo == 0 and (tt // dil) % min(cfg["attn_sub"], tt // dil) == 0
        halo_per_tile = tt // halo
        halo_per_seq = seq // halo
        h0 = g * HEADS_PER_GROUP

        def main(cb, h0=h0):
            return pl.BlockSpec((1, tt, LANES), lambda b, h, i: (cb + h0 + h, b * n_t + i, 0))

        def prev(cb, h0=h0, halo=halo, hpt=halo_per_tile, hps=halo_per_seq):
            return pl.BlockSpec((1, halo, LANES),
                                lambda b, h, i: (cb + h0 + h, b * hps + jnp.maximum(i * hpt - 1, 0), 0))

        def nxt(cb, h0=h0, halo=halo, hpt=halo_per_tile, hps=halo_per_seq):
            return pl.BlockSpec((1, halo, LANES),
                                lambda b, h, i: (cb + h0 + h, b * hps + jnp.minimum((i + 1) * hpt, hps - 1), 0))

        in_specs += [main(CB_Q), prev(CB_K), main(CB_K), nxt(CB_K), prev(CB_V), main(CB_V), nxt(CB_V)]
    n_groups = len(ATTN_PATTERNS)
    kern = functools.partial(_attn_kernel, seq=seq, tt=tt, sub_max=cfg["attn_sub"])
    return pl.pallas_call(
        kern,
        grid=(batch, HEADS_PER_GROUP, n_t),
        in_specs=in_specs,
        out_specs=pl.BlockSpec((tt, LANES), lambda b, h, i: (b * n_t + i, h)),
        out_shape=jax.ShapeDtypeStruct((T, ATTN_OUT_W), BF16),
        scratch_shapes=[pltpu.VMEM((n_groups, tt, LANES), F32), pltpu.VMEM((n_groups, tt, LANES), F32)],
        compiler_params=_params("parallel", "parallel", "parallel"),
        name="attn",
    )(*([proj] * len(in_specs)))


def _merge_kernel(h0_ref, h1_ref, g_ref, gr_ref, ga_ref, attn_ref, wr_ref, wa_ref, out_ref):
    tm = out_ref.shape[0]
    hg = []
    for j in range(RNN_BLOCKS):
        grp, c = divmod(j, SUBLANES)
        h = (h0_ref, h1_ref)[grp][0, pl.ds(c, tm, stride=SUBLANES), :]
        hg.append((h * jax.nn.gelu(g_ref[j])).astype(BF16))
    y_rnn = jnp.dot(jnp.concatenate(hg, axis=-1), wr_ref[...], preferred_element_type=F32)
    y_attn = jnp.dot(attn_ref[...], wa_ref[...], preferred_element_type=F32)
    for j in range(RNN_BLOCKS):
        sl = slice(j * LANES, (j + 1) * LANES)
        mixed = _sigmoid(gr_ref[j]) * y_rnn[:, sl] + _sigmoid(ga_ref[j]) * y_attn[:, sl]
        out_ref[:, sl] = mixed.astype(BF16)


def _merge(proj, hrnn, attn, w_rnn_proj, w_attn_proj, cfg):
    T = proj.shape[1]
    tm = cfg["merge_tm"]
    blk16 = lambda cb: pl.BlockSpec((RNN_BLOCKS, tm, LANES), lambda m: (cb // RNN_BLOCKS, m, 0))
    hblk = lambda grp: pl.BlockSpec((1, tm * SUBLANES, LANES), lambda m: (grp, m, 0))
    tok = lambda w: pl.BlockSpec((tm, w), lambda m: (m, 0))
    const = lambda a: pl.BlockSpec(a.shape, lambda m: (0, 0), pipeline_mode=pl.Buffered(1))
    return pl.pallas_call(
        _merge_kernel,
        grid=(T // tm,),
        in_specs=[hblk(0), hblk(1),
                  blk16(CB_GRNN), blk16(CB_GATE_RNN), blk16(CB_GATE_ATTN),
                  tok(ATTN_OUT_W), const(w_rnn_proj), const(w_attn_proj)],
        out_specs=tok(D_MODEL),
        out_shape=jax.ShapeDtypeStruct((T, D_MODEL), BF16),
        compiler_params=_params("parallel"),
        name="merge",
    )(hrnn, hrnn, proj, proj, proj, attn, w_rnn_proj, w_attn_proj)


def _mlp_kernel(x_ref, mix_ref, wo_ref, g_ref, fg_ref, wu_ref, wd_ref, o_ref, xn_ref, acc_ref, act_ref):
    f = pl.program_id(1)
    tm = acc_ref.shape[0]
    col_groups = lambda width: [slice(j * MXU_COLS, (j + 1) * MXU_COLS) for j in range(width // MXU_COLS)]

    @pl.when(f == 0)
    def _():
        sq = jnp.zeros((tm, LANES), F32)
        for cols in col_groups(D_MODEL):
            h1 = x_ref[:, cols] + jnp.dot(mix_ref[...], wo_ref[:, cols], preferred_element_type=F32)
            acc_ref[:, cols] = h1
            for k in range(MXU_COLS // LANES):
                part = h1[:, k * LANES:(k + 1) * LANES]
                sq = sq + part * part
        inv = lax.rsqrt(jnp.sum(sq, axis=-1, keepdims=True) * (1.0 / D_MODEL) + EPS)
        for cols in col_groups(D_MODEL):
            xn_ref[:, cols] = (acc_ref[:, cols] * inv * g_ref[:, cols]).astype(BF16)

    for cols in col_groups(act_ref.shape[1]):
        u = jnp.dot(xn_ref[...], wu_ref[:, cols], preferred_element_type=F32)
        act_ref[:, cols] = jnp.square(jnp.maximum(u, 0.0)).astype(BF16)
    for cols in col_groups(D_MODEL):
        acc_ref[:, cols] += jnp.dot(act_ref[...], wd_ref[:, cols], preferred_element_type=F32)

    @pl.when(f == pl.num_programs(1) - 1)
    def _():
        o_ref[...] = _rms(acc_ref[...], fg_ref[...])


def _mlp(x2, mixed, w_out, mlp_g, final_g, w_up, w_down, cfg):
    T = x2.shape[0]
    tm, tf = cfg["mlp_tm"], cfg["mlp_tf"]
    tok = pl.BlockSpec((tm, D_MODEL), lambda m, f: (m, 0))
    vec = pl.BlockSpec((1, D_MODEL), lambda m, f: (0, 0))
    return pl.pallas_call(
        _mlp_kernel,
        grid=(T // tm, D_FF // tf),
        in_specs=[
            tok, tok,
            pl.BlockSpec((D_MODEL, D_MODEL), lambda m, f: (0, 0), pipeline_mode=pl.Buffered(1)),
            vec, vec,
            pl.BlockSpec((D_MODEL, tf), lambda m, f: (0, f)),
            pl.BlockSpec((tf, D_MODEL), lambda m, f: (f, 0)),
        ],
        out_specs=tok,
        out_shape=jax.ShapeDtypeStruct((T, D_MODEL), F32),
        scratch_shapes=[pltpu.VMEM((tm, D_MODEL), BF16), pltpu.VMEM((tm, D_MODEL), F32),
                        pltpu.VMEM((tm, tf), BF16)],
        compiler_params=_params("parallel", "arbitrary"),
        name="mlp",
    )(x2, mixed, w_out, mlp_g, final_g, w_up, w_down)


def _rope_lane_perm():
    dims = np.arange(HEAD_DIM)
    rest = dims[ROPE_DIM:]
    n_mid = LANES // 2 - ROPE_HALF
    return np.concatenate([dims[:ROPE_HALF], rest[:n_mid], dims[ROPE_HALF:ROPE_DIM], rest[n_mid:]])


def _rope_tables(seq):
    inv_freq = ROPE_THETA ** (-np.arange(0, ROPE_DIM, 2, dtype=np.float64) / ROPE_DIM)
    ang = np.arange(seq, dtype=np.float64)[:, None] * inv_freq[None, :]
    cos, sin = np.cos(ang), np.sin(ang)
    pad = LANES // 2 - ROPE_HALF
    ones, zeros = np.ones((seq, pad)), np.zeros((seq, pad))
    cos_t = np.concatenate([cos, ones, cos, ones], axis=-1)
    sin_t = np.concatenate([-sin, zeros, sin, zeros], axis=-1)
    return jnp.asarray(cos_t, F32), jnp.asarray(sin_t, F32)


def _layer(x, wts):
    batch, seq, _ = x.shape
    cfg = _tiles(seq)
    x2 = x.reshape(batch * seq, D_MODEL)
    cos_t, sin_t = _rope_tables(seq)
    proj = _inproj(x2, wts["mix_g"], wts["w_in"], cos_t, sin_t, seq, cfg)
    rnn_args = (proj, wts["conv_w"], wts["conv_b"], wts["wg"], wts["bg"], wts["lam"], batch, seq, cfg)
    hrnn = _rglru(*rnn_args, reverse=True, other=_rglru(*rnn_args, reverse=False))
    attn = _attention(proj, batch, seq, cfg)
    mixed = _merge(proj, hrnn, attn, wts["w_rnn_proj"], wts["w_attn_proj"], cfg)
    y = _mlp(x2, mixed, wts["w_out"], wts["mlp_g"], wts["final_g"], wts["w_up"], wts["w_down"], cfg)
    return y.reshape(batch, seq, D_MODEL)


def kernel(x_prompt, x_sample, mix_norm_g, w_in, conv_w, conv_b, lru_w_a, lru_b_a, lru_w_i, lru_b_i,
           lru_lambda, w_rnn_proj, w_attn_proj, w_out, mlp_norm_g, w_up, w_down, final_norm_g):
    assert mix_norm_g.shape[0] == 1, "single-layer stack"
    for window, dil in ATTN_PATTERNS:
        assert window // (2 * dil) == ATTN_RADIUS
    l = 0
    c0 = 2 * D_RNN
    c1 = c0 + 3 * ATTN_W
    w = w_in[l]
    perm = _rope_lane_perm()
    qk = w[:, c0:c0 + 2 * ATTN_W].reshape(D_MODEL, 2 * N_HEADS, HEAD_DIM)[:, :, perm].reshape(D_MODEL, 2 * ATTN_W)
    w_in_p = jnp.concatenate([w[:, :c0], w[:, c1:], qk, w[:, c0 + 2 * ATTN_W:c1]], axis=1).astype(BF16)
    nb = RNN_BLOCKS
    bw = D_RNN // nb
    wts = dict(
        mix_g=mix_norm_g[l][None, :],
        w_in=w_in_p,
        conv_w=conv_w[l].reshape(CONV_W, nb, bw).transpose(1, 0, 2),
        conv_b=conv_b[l].reshape(nb, 1, bw),
        wg=(0.5 * jnp.concatenate([lru_w_a[l], lru_w_i[l]], axis=-1)).astype(BF16),
        bg=0.5 * jnp.concatenate([lru_b_a[l].reshape(2, nb, 1, bw), lru_b_i[l].reshape(2, nb, 1, bw)], axis=-1),
        lam=lru_lambda[l].reshape(2, nb, 1, bw),
        w_rnn_proj=w_rnn_proj[l].astype(BF16),
        w_attn_proj=w_attn_proj[l].astype(BF16),
        w_out=w_out[l].astype(BF16),
        mlp_g=mlp_norm_g[l][None, :],
        final_g=final_norm_g[None, :],
        w_up=w_up[l].astype(BF16),
        w_down=w_down[l].astype(BF16),
    )
    return (_layer(x_prompt, wts), _layer(x_sample, wts))
```

```python
import functools

import numpy as np
import jax
import jax.numpy as jnp
from jax import lax
from jax.experimental import pallas as pl
from jax.experimental.pallas import tpu as pltpu

F32 = jnp.float32
BF16 = jnp.bfloat16

LANES = 128
SUBLANES = 8
MXU_COLS = 256
VMEM_LIMIT = 56 * 1024 * 1024

D_MODEL = 2048
D_RNN = D_MODEL
RNN_BLOCKS = 16
RNN_GROUPS = RNN_BLOCKS // SUBLANES
SCAN_UNROLL = 8
CONV_W = 4
LRU_C = 8.0
HEAD_DIM = 128
ATTN_PATTERNS = ((128, 1), (512, 4), (2048, 16))
HEADS_PER_GROUP = 4
N_HEADS = len(ATTN_PATTERNS) * HEADS_PER_GROUP
ATTN_W = N_HEADS * HEAD_DIM
ATTN_OUT_W = HEADS_PER_GROUP * HEAD_DIM
ATTN_RADIUS = 64
ROPE_DIM = HEAD_DIM // 4
ROPE_HALF = ROPE_DIM // 2
ROPE_THETA = 500000.0
D_FF = 4 * D_MODEL
EPS = 1e-6
NEG_INF = -1e30

CB_XRNN = 0
CB_GRNN = 16
CB_GATE_RNN = 32
CB_GATE_ATTN = 48
CB_Q = 64
CB_K = CB_Q + N_HEADS
CB_V = CB_K + N_HEADS
IN_BLOCKS = CB_V + N_HEADS
IN_COLS = IN_BLOCKS * LANES


def _tiles(seq):
    return dict(
        proj_tm=min(1024, seq), proj_tn=1280,
        rnn_chunk=min(512, seq),
        attn_tokens=min(2048, seq), attn_sub=128,
        merge_tm=min(512, seq),
        mlp_tm=min(512, seq), mlp_tf=1024,
    )


def _params(*sem):
    return pltpu.CompilerParams(dimension_semantics=sem, vmem_limit_bytes=VMEM_LIMIT)


def _rms(xf, g):
    return xf * lax.rsqrt(jnp.mean(xf * xf, axis=-1, keepdims=True) + EPS) * g


def _sigmoid(z):
    return 0.5 * jnp.tanh(0.5 * z) + 0.5


def _inproj_kernel(x_ref, g_ref, w_ref, cos_ref, sin_ref, o_ref, xn_ref):
    n = pl.program_id(1)

    @pl.when(n == 0)
    def _():
        xn_ref[...] = _rms(x_ref[...], g_ref[...]).astype(BF16)

    nblk = o_ref.shape[0]
    tiles = {}
    for tile in range(IN_BLOCKS // nblk):
        flags = tuple(CB_Q <= tile * nblk + j < CB_V for j in range(nblk))
        tiles.setdefault(flags, []).append(tile)

    for flags, members in tiles.items():
        cond = functools.reduce(jnp.logical_or, [n == t for t in members])

        @pl.when(cond)
        def _(flags=flags):
            for j2 in range(nblk // 2):
                cols = slice(2 * j2 * LANES, 2 * (j2 + 1) * LANES)
                r = jnp.dot(xn_ref[...], w_ref[:, cols], preferred_element_type=F32)
                for half in range(2):
                    j = 2 * j2 + half
                    t = r[:, half * LANES:(half + 1) * LANES]
                    if flags[j]:
                        t = t * cos_ref[...] + pltpu.roll(t, LANES // 2, 1) * sin_ref[...]
                    o_ref[j] = t


def _inproj(x2, g, w_in_p, cos_t, sin_t, seq, cfg):
    T = x2.shape[0]
    tm, tn = cfg["proj_tm"], cfg["proj_tn"]
    nb = tn // LANES
    assert IN_BLOCKS % nb == 0 and nb % 2 == 0
    tiles_per_seq = seq // tm
    kern = _inproj_kernel
    return pl.pallas_call(
        kern,
        grid=(T // tm, IN_COLS // tn),
        in_specs=[
            pl.BlockSpec((tm, D_MODEL), lambda m, n: (m, 0)),
            pl.BlockSpec((1, D_MODEL), lambda m, n: (0, 0)),
            pl.BlockSpec((D_MODEL, tn), lambda m, n: (0, n)),
            pl.BlockSpec((tm, LANES), lambda m, n: (m % tiles_per_seq, 0)),
            pl.BlockSpec((tm, LANES), lambda m, n: (m % tiles_per_seq, 0)),
        ],
        out_specs=pl.BlockSpec((nb, tm, LANES), lambda m, n: (n, m, 0)),
        out_shape=jax.ShapeDtypeStruct((IN_BLOCKS, T, LANES), F32),
        scratch_shapes=[pltpu.VMEM((tm, D_MODEL), BF16)],
        compiler_params=_params("parallel", "arbitrary"),
        name="inproj",
    )(x2, g, w_in_p, cos_t, sin_t)


def _rglru_kernel(*refs, chunk, n_chunks, reverse):
    x_ref, xp_ref, xn_ref, cw_ref, cb_ref, wg_ref, bg_ref, lam_ref = refs[:8]
    other_ref = refs[8] if reverse else None
    h_ref, a_scr, u_scr, state_ref = refs[-4:]
    ci = pl.program_id(1)
    c_seq = n_chunks - 1 - ci if reverse else ci
    ext_rows = chunk + 2 * SUBLANES
    mid = slice(SUBLANES, SUBLANES + chunk)

    @pl.when(ci == 0)
    def _():
        state_ref[...] = jnp.zeros_like(state_ref)

    def gate_block(c, carry):
        x = x_ref[c]
        prev = jnp.where(c_seq == 0, 0.0, xp_ref[c])
        nxt = jnp.where(c_seq == n_chunks - 1, 0.0, xn_ref[c])
        cw, cb, wg, bg, lam = cw_ref[c], cb_ref[c], wg_ref[0, c], bg_ref[0, c], lam_ref[0, c]
        ext = jnp.concatenate([prev, x, nxt], axis=0)
        xc = (cb + pltpu.roll(ext, 2, 0)[mid] * cw[0:1] + pltpu.roll(ext, 1, 0)[mid] * cw[1:2]
              + x * cw[2:3] + pltpu.roll(ext, ext_rows - 1, 0)[mid] * cw[3:4])
        tz = jnp.tanh(jnp.dot(xc.astype(BF16), wg, preferred_element_type=F32) + bg)
        i = 0.5 * tz[:, LANES:] + 0.5
        z = -lam
        softplus = jnp.maximum(z, 0.0) + jnp.log1p(jnp.exp(-jnp.abs(z)))
        half_rate = (-0.5 * LRU_C) * softplus
        log_a = half_rate * tz[:, :LANES] + half_rate
        a = jnp.exp(log_a)
        u = jnp.sqrt(-jnp.tanh(log_a) * (a * a + 1.0)) * (i * xc)
        grp, sub = c // SUBLANES, c % SUBLANES
        a_scr[grp, pl.ds(sub, chunk, stride=SUBLANES), :] = a
        u_scr[grp, pl.ds(sub, chunk, stride=SUBLANES), :] = u
        return carry

    lax.fori_loop(0, RNN_BLOCKS, gate_block, 0)

    def steps(kb, hs):
        hs = list(hs)
        ts = [kb * SCAN_UNROLL + k for k in range(SCAN_UNROLL)]
        rows = [pl.ds(pl.multiple_of((chunk - 1 - t if reverse else t) * SUBLANES, SUBLANES), SUBLANES) for t in ts]
        groups = range(RNN_GROUPS)
        a = [[a_scr[grp, r, :] for r in rows] for grp in groups]
        u = [[u_scr[grp, r, :] for r in rows] for grp in groups]
        other = [[other_ref[grp, r, :] for r in rows] for grp in groups] if reverse else None
        for k, r in enumerate(rows):
            for grp in groups:
                hs[grp] = a[grp][k] * hs[grp] + u[grp][k]
                h_ref[grp, r, :] = hs[grp] + other[grp][k] if reverse else hs[grp]
        return tuple(hs)

    hs = lax.fori_loop(0, chunk // SCAN_UNROLL, steps, tuple(state_ref[grp] for grp in range(RNN_GROUPS)))
    for grp, h in enumerate(hs):
        state_ref[grp] = h


def _rglru(proj, conv_w, conv_b, wg, bg, lam, batch, seq, cfg, reverse, other=None):
    T = proj.shape[1]
    chunk = cfg["rnn_chunk"]
    n_chunks = seq // chunk
    halo_per_chunk = chunk // SUBLANES
    halo_per_seq = seq // SUBLANES
    d = int(reverse)

    def pos(ci):
        return n_chunks - 1 - ci if reverse else ci

    per_block = lambda shape: pl.BlockSpec((RNN_BLOCKS,) + shape, lambda b, ci: (0,) * (1 + len(shape)))
    per_dir_block = lambda shape: pl.BlockSpec((1, RNN_BLOCKS) + shape, lambda b, ci: (d,) + (0,) * (1 + len(shape)))
    state_spec = pl.BlockSpec((RNN_GROUPS, chunk * SUBLANES, LANES), lambda b, ci: (0, b * n_chunks + pos(ci), 0))
    kern = functools.partial(_rglru_kernel, chunk=chunk, n_chunks=n_chunks, reverse=reverse)
    return pl.pallas_call(
        kern,
        grid=(batch, n_chunks),
        in_specs=[
            pl.BlockSpec((RNN_BLOCKS, chunk, LANES), lambda b, ci: (0, b * n_chunks + pos(ci), 0)),
            pl.BlockSpec((RNN_BLOCKS, SUBLANES, LANES),
                         lambda b, ci: (0, b * halo_per_seq + jnp.maximum(pos(ci) * halo_per_chunk - 1, 0), 0)),
            pl.BlockSpec((RNN_BLOCKS, SUBLANES, LANES),
                         lambda b, ci: (0, b * halo_per_seq
                                        + jnp.minimum((pos(ci) + 1) * halo_per_chunk, halo_per_seq - 1), 0)),
            per_block((CONV_W, LANES)),
            per_block((1, LANES)),
            per_dir_block((LANES, 2 * LANES)),
            per_dir_block((1, 2 * LANES)),
            per_dir_block((1, LANES)),
        ] + ([state_spec] if reverse else []),
        out_specs=state_spec,
        out_shape=jax.ShapeDtypeStruct((RNN_GROUPS, T * SUBLANES, LANES), F32),
        scratch_shapes=[pltpu.VMEM((RNN_GROUPS, chunk * SUBLANES, LANES), F32)] * 2
                       + [pltpu.VMEM((RNN_GROUPS, SUBLANES, LANES), F32)],
        compiler_params=_params("parallel", "arbitrary"),
        name="rglru_bwd" if reverse else "rglru_fwd",
    )(proj, proj, proj, conv_w, conv_b, wg, bg, lam, *([other] if reverse else []))


ATTN_REFS_PER_GROUP = 7
ATTN_COMBINE_ROWS = 256


def _attn_kernel(*refs, seq, tt, sub_max):
    n_groups = len(ATTN_PATTERNS)
    in_refs = refs[:ATTN_REFS_PER_GROUP * n_groups]
    out_ref, o_scr, l_scr = refs[ATTN_REFS_PER_GROUP * n_groups:]
    i = pl.program_id(2)
    scale = 1.0 / float(np.sqrt(HEAD_DIM))

    for g, (_, dil) in enumerate(ATTN_PATTERNS):
        q_ref, kp_ref, km_ref, kn_ref, vp_ref, vm_ref, vn_ref = in_refs[ATTN_REFS_PER_GROUP * g:ATTN_REFS_PER_GROUP * (g + 1)]
        tq = tt // dil
        sub = min(sub_max, tq)
        sub_len = seq // dil
        nk = sub + 2 * ATTN_RADIUS
        qi = lax.broadcasted_iota(jnp.int32, (sub, nk), 0)
        kj = lax.broadcasted_iota(jnp.int32, (sub, nk), 1)
        band = jnp.abs(kj - ATTN_RADIUS - qi) <= ATTN_RADIUS

        def rows(ref, r, n, dil=dil):
            return ref[0, pl.ds(r, n, stride=dil), :] if dil > 1 else ref[0, pl.ds(r, n), :]

        def ext(p_ref, m_ref, n_ref, r, tq=tq, rows=rows):
            parts = [rows(p_ref, r, ATTN_RADIUS), rows(m_ref, r, tq), rows(n_ref, r, ATTN_RADIUS)]
            return jnp.concatenate(parts, axis=0).astype(BF16)

        for r in range(dil):
            q = rows(q_ref, r, tq).astype(BF16)
            kext = ext(kp_ref, km_ref, kn_ref, r)
            vext = ext(vp_ref, vm_ref, vn_ref, r)
            for sb in range(tq // sub):
                r0 = sb * sub
                s = lax.dot_general(q[r0:r0 + sub], kext[r0:r0 + nk], (((1,), (1,)), ((), ())),
                                    preferred_element_type=F32) * scale
                kpos = i * tq + (r0 - ATTN_RADIUS) + kj
                mask = band & (kpos >= 0) & (kpos < sub_len)
                s = jnp.where(mask, s, NEG_INF)
                m = jnp.max(s, axis=-1, keepdims=True)
                p = jnp.exp(s - m)
                den = jnp.sum(p, axis=-1, keepdims=True)
                o = jnp.dot(p.astype(BF16), vext[r0:r0 + nk], preferred_element_type=F32) / den
                lse = jnp.broadcast_to(m + jnp.log(den), (sub, LANES))
                dst = pl.ds(r + r0 * dil, sub, stride=dil) if dil > 1 else pl.ds(r0, sub)
                o_scr[g, dst, :] = o
                l_scr[g, dst, :] = lse

    rows_c = min(ATTN_COMBINE_ROWS, tt)

    def combine(k, carry):
        sl = pl.ds(pl.multiple_of(k * rows_c, rows_c), rows_c)
        ls = [l_scr[g, sl, :] for g in range(n_groups)]
        mx = functools.reduce(jnp.maximum, ls)
        es = [jnp.exp(l - mx) for l in ls]
        num = sum(e * o_scr[g, sl, :] for g, e in enumerate(es))
        out_ref[sl, :] = (num / sum(es)).astype(BF16)
        return carry

    lax.fori_loop(0, tt // rows_c, combine, 0)


def _attention(proj, batch, seq, cfg):
    T = proj.shape[1]
    tt = cfg["attn_tokens"]
    n_t = seq // tt
    assert seq % tt == 0
    in_specs = []
    for g, (_, dil) in enumerate(ATTN_PATTERNS):
        halo = ATTN_RADIUS * dil
        assert tt % halo == 0 and (tt // dil) % min(cfg["attn_sub"], tt // dil) == 0
        halo_per_tile = tt // halo
        halo_per_seq = seq // halo
        h0 = g * HEADS_PER_GROUP

        def main(cb, h0=h0):
            return pl.BlockSpec((1, tt, LANES), lambda b, h, i: (cb + h0 + h, b * n_t + i, 0))

        def prev(cb, h0=h0, halo=halo, hpt=halo_per_tile, hps=halo_per_seq):
            return pl.BlockSpec((1, halo, LANES),
                                lambda b, h, i: (cb + h0 + h, b * hps + jnp.maximum(i * hpt - 1, 0), 0))

        def nxt(cb, h0=h0, halo=halo, hpt=halo_per_tile, hps=halo_per_seq):
            return pl.BlockSpec((1, halo, LANES),
                                lambda b, h, i: (cb + h0 + h, b * hps + jnp.minimum((i + 1) * hpt, hps - 1), 0))

        in_specs += [main(CB_Q), prev(CB_K), main(CB_K), nxt(CB_K), prev(CB_V), main(CB_V), nxt(CB_V)]
    n_groups = len(ATTN_PATTERNS)
    kern = functools.partial(_attn_kernel, seq=seq, tt=tt, sub_max=cfg["attn_sub"])
    return pl.pallas_call(
        kern,
        grid=(batch, HEADS_PER_GROUP, n_t),
        in_specs=in_specs,
        out_specs=pl.BlockSpec((tt, LANES), lambda b, h, i: (b * n_t + i, h)),
        out_shape=jax.ShapeDtypeStruct((T, ATTN_OUT_W), BF16),
        scratch_shapes=[pltpu.VMEM((n_groups, tt, LANES), F32), pltpu.VMEM((n_groups, tt, LANES), F32)],
        compiler_params=_params("parallel", "parallel", "parallel"),
        name="attn",
    )(*([proj] * len(in_specs)))


def _merge_kernel(h0_ref, h1_ref, g_ref, gr_ref, ga_ref, attn_ref, wr_ref, wa_ref, out_ref):
    tm = out_ref.shape[0]
    hg = []
    for j in range(RNN_BLOCKS):
        grp, c = divmod(j, SUBLANES)
        h = (h0_ref, h1_ref)[grp][0, pl.ds(c, tm, stride=SUBLANES), :]
        hg.append((h * jax.nn.gelu(g_ref[j])).astype(BF16))
    y_rnn = jnp.dot(jnp.concatenate(hg, axis=-1), wr_ref[...], preferred_element_type=F32)
    y_attn = jnp.dot(attn_ref[...], wa_ref[...], preferred_element_type=F32)
    for j in range(RNN_BLOCKS):
        sl = slice(j * LANES, (j + 1) * LANES)
        mixed = _sigmoid(gr_ref[j]) * y_rnn[:, sl] + _sigmoid(ga_ref[j]) * y_attn[:, sl]
        out_ref[:, sl] = mixed.astype(BF16)


def _merge(proj, hrnn, attn, w_rnn_proj, w_attn_proj, cfg):
    T = proj.shape[1]
    tm = cfg["merge_tm"]
    blk16 = lambda cb: pl.BlockSpec((RNN_BLOCKS, tm, LANES), lambda m: (cb // RNN_BLOCKS, m, 0))
    hblk = lambda grp: pl.BlockSpec((1, tm * SUBLANES, LANES), lambda m: (grp, m, 0))
    tok = lambda w: pl.BlockSpec((tm, w), lambda m: (m, 0))
    const = lambda a: pl.BlockSpec(a.shape, lambda m: (0, 0), pipeline_mode=pl.Buffered(1))
    return pl.pallas_call(
        _merge_kernel,
        grid=(T // tm,),
        in_specs=[hblk(0), hblk(1),
                  blk16(CB_GRNN), blk16(CB_GATE_RNN), blk16(CB_GATE_ATTN),
                  tok(ATTN_OUT_W), const(w_rnn_proj), const(w_attn_proj)],
        out_specs=tok(D_MODEL),
        out_shape=jax.ShapeDtypeStruct((T, D_MODEL), BF16),
        compiler_params=_params("parallel"),
        name="merge",
    )(hrnn, hrnn, proj, proj, proj, attn, w_rnn_proj, w_attn_proj)


def _mlp_kernel(x_ref, mix_ref, wo_ref, g_ref, fg_ref, wu_ref, wd_ref, o_ref, xn_ref, acc_ref, act_ref):
    f = pl.program_id(1)
    tm = acc_ref.shape[0]
    col_groups = lambda width: [slice(j * MXU_COLS, (j + 1) * MXU_COLS) for j in range(width // MXU_COLS)]

    @pl.when(f == 0)
    def _():
        sq = jnp.zeros((tm, LANES), F32)
        for cols in col_groups(D_MODEL):
            h1 = x_ref[:, cols] + jnp.dot(mix_ref[...], wo_ref[:, cols], preferred_element_type=F32)
            acc_ref[:, cols] = h1
            for k in range(MXU_COLS // LANES):
                part = h1[:, k * LANES:(k + 1) * LANES]
                sq = sq + part * part
        inv = lax.rsqrt(jnp.sum(sq, axis=-1, keepdims=True) * (1.0 / D_MODEL) + EPS)
        for cols in col_groups(D_MODEL):
            xn_ref[:, cols] = (acc_ref[:, cols] * inv * g_ref[:, cols]).astype(BF16)

    for cols in col_groups(act_ref.shape[1]):
        u = jnp.dot(xn_ref[...], wu_ref[:, cols], preferred_element_type=F32)
        act_ref[:, cols] = jnp.square(jnp.maximum(u, 0.0)).astype(BF16)
    for cols in col_groups(D_MODEL):
        acc_ref[:, cols] += jnp.dot(act_ref[...], wd_ref[:, cols], preferred_element_type=F32)

    @pl.when(f == pl.num_programs(1) - 1)
    def _():
        o_ref[...] = _rms(acc_ref[...], fg_ref[...])


def _mlp(x2, mixed, w_out, mlp_g, final_g, w_up, w_down, cfg):
    T = x2.shape[0]
    tm, tf = cfg["mlp_tm"], cfg["mlp_tf"]
    tok = pl.BlockSpec((tm, D_MODEL), lambda m, f: (m, 0))
    vec = pl.BlockSpec((1, D_MODEL), lambda m, f: (0, 0))
    return pl.pallas_call(
        _mlp_kernel,
        grid=(T // tm, D_FF // tf),
        in_specs=[
            tok, tok,
            pl.BlockSpec((D_MODEL, D_MODEL), lambda m, f: (0, 0), pipeline_mode=pl.Buffered(1)),
            vec, vec,
            pl.BlockSpec((D_MODEL, tf), lambda m, f: (0, f)),
            pl.BlockSpec((tf, D_MODEL), lambda m, f: (f, 0)),
        ],
        out_specs=tok,
        out_shape=jax.ShapeDtypeStruct((T, D_MODEL), F32),
        scratch_shapes=[pltpu.VMEM((tm, D_MODEL), BF16), pltpu.VMEM((tm, D_MODEL), F32),
                        pltpu.VMEM((tm, tf), BF16)],
        compiler_params=_params("parallel", "arbitrary"),
        name="mlp",
    )(x2, mixed, w_out, mlp_g, final_g, w_up, w_down)


def _rope_lane_perm():
    dims = np.arange(HEAD_DIM)
    rest = dims[ROPE_DIM:]
    n_mid = LANES // 2 - ROPE_HALF
    return np.concatenate([dims[:ROPE_HALF], rest[:n_mid], dims[ROPE_HALF:ROPE_DIM], rest[n_mid:]])


def _rope_tables(seq):
    inv_freq = ROPE_THETA ** (-np.arange(0, ROPE_DIM, 2, dtype=np.float64) / ROPE_DIM)
    ang = np.arange(seq, dtype=np.float64)[:, None] * inv_freq[None, :]
    cos, sin = np.cos(ang), np.sin(ang)
    pad = LANES // 2 - ROPE_HALF
    ones, zeros = np.ones((seq, pad)), np.zeros((seq, pad))
    cos_t = np.concatenate([cos, ones, cos, ones], axis=-1)
    sin_t = np.concatenate([-sin, zeros, sin, zeros], axis=-1)
    return jnp.asarray(cos_t, F32), jnp.asarray(sin_t, F32)


def _layer(x, wts):
    batch, seq, _ = x.shape
    cfg = _tiles(seq)
    x2 = x.reshape(batch * seq, D_MODEL)
    cos_t, sin_t = _rope_tables(seq)
    proj = _inproj(x2, wts["mix_g"], wts["w_in"], cos_t, sin_t, seq, cfg)
    rnn_args = (proj, wts["conv_w"], wts["conv_b"], wts["wg"], wts["bg"], wts["lam"], batch, seq, cfg)
    hrnn = _rglru(*rnn_args, reverse=True, other=_rglru(*rnn_args, reverse=False))
    attn = _attention(proj, batch, seq, cfg)
    mixed = _merge(proj, hrnn, attn, wts["w_rnn_proj"], wts["w_attn_proj"], cfg)
    y = _mlp(x2, mixed, wts["w_out"], wts["mlp_g"], wts["final_g"], wts["w_up"], wts["w_down"], cfg)
    return y.reshape(batch, seq, D_MODEL)


def kernel(x_prompt, x_sample, mix_norm_g, w_in, conv_w, conv_b, lru_w_a, lru_b_a, lru_w_i, lru_b_i,
           lru_lambda, w_rnn_proj, w_attn_proj, w_out, mlp_norm_g, w_up, w_down, final_norm_g):
    assert mix_norm_g.shape[0] == 1, "single-layer stack"
    for window, dil in ATTN_PATTERNS:
        assert window // (2 * dil) == ATTN_RADIUS
    l = 0
    c0 = 2 * D_RNN
    c1 = c0 + 3 * ATTN_W
    w = w_in[l]
    perm = _rope_lane_perm()
    qk = w[:, c0:c0 + 2 * ATTN_W].reshape(D_MODEL, 2 * N_HEADS, HEAD_DIM)[:, :, perm].reshape(D_MODEL, 2 * ATTN_W)
    w_in_p = jnp.concatenate([w[:, :c0], w[:, c1:], qk, w[:, c0 + 2 * ATTN_W:c1]], axis=1).astype(BF16)
    nb = RNN_BLOCKS
    bw = D_RNN // nb
    wts = dict(
        mix_g=mix_norm_g[l][None, :],
        w_in=w_in_p,
        conv_w=conv_w[l].reshape(CONV_W, nb, bw).transpose(1, 0, 2),
        conv_b=conv_b[l].reshape(nb, 1, bw),
        wg=(0.5 * jnp.concatenate([lru_w_a[l], lru_w_i[l]], axis=-1)).astype(BF16),
        bg=0.5 * jnp.concatenate([lru_b_a[l].reshape(2, nb, 1, bw), lru_b_i[l].reshape(2, nb, 1, bw)], axis=-1),
        lam=lru_lambda[l].reshape(2, nb, 1, bw),
        w_rnn_proj=w_rnn_proj[l].astype(BF16),
        w_attn_proj=w_attn_proj[l].astype(BF16),
        w_out=w_out[l].astype(BF16),
        mlp_g=mlp_norm_g[l][None, :],
        final_g=final_norm_g[None, :],
        w_up=w_up[l].astype(BF16),
        w_down=w_down[l].astype(BF16),
    )
    return (_layer(x_prompt, wts), _layer(x_sample, wts))
```
